```python
import math
import jax
import jax.numpy as jnp
from jax import lax
import numpy as np

D_MODEL = 1024
BATCH = 2
SEQ = 8192
DEPTH = 2

NORM_EPS = 1e-6
POOL_WINDOWS = (2, 4, 8, 16)
POOL_GROUPS = 4
POOL_WIDTH = D_MODEL // 2
POOL_GROUP_DIM = POOL_WIDTH // POOL_GROUPS
ATTN_HEADS = 8
ATTN_HEAD_DIM = 64
ATTN_WIDTH = ATTN_HEADS * ATTN_HEAD_DIM
Q_BLOCK = 128
SSD_HEAD_DIM = 64
SSD_WIDTH = D_MODEL
SSD_HEADS = SSD_WIDTH // SSD_HEAD_DIM
SSD_GROUPS = 2
SSD_STATE = 128
SSD_CONV = 4
SSD_CHUNK = 128
SSD_CONV_CH = SSD_WIDTH + 2 * SSD_GROUPS * SSD_STATE
N_BRANCH = 3
FFN_DIM = 2816
FFN_CONV = 3
IN_SPLITS = (POOL_WIDTH, ATTN_WIDTH, ATTN_WIDTH, ATTN_WIDTH, ATTN_HEADS,
             SSD_WIDTH, SSD_CONV_CH, SSD_HEADS, N_BRANCH * D_MODEL)
IN_TOTAL = sum(IN_SPLITS)

kernel_name = "hybrid_pool_fox_ssd_block"


def rmsnorm(x, w):
    xf = x.astype(jnp.float32)
    y = xf * lax.rsqrt(jnp.mean(xf * xf, axis=-1, keepdims=True) + NORM_EPS)
    return (y * w.astype(jnp.float32)).astype(x.dtype)


def split_columns(proj):
    offsets, acc = [], 0
    for n in IN_SPLITS[:-1]:
        acc += n
        offsets.append(acc)
    return jnp.split(proj, offsets, axis=-1)


def causal_dwconv(x, w, b):
    K, C = w.shape
    y = lax.conv_general_dilated(
        x, w[:, None, :].astype(x.dtype), window_strides=(1,),
        padding=[(K - 1, 0)], dimension_numbers=('NWC', 'WIO', 'NWC'),
        feature_group_count=C)
    return y + b.astype(x.dtype)


def pool_mixer(v, mix_w, scale):
    B_, S, _ = v.shape
    vf = v.astype(jnp.float32).reshape(B_, S, POOL_GROUPS, POOL_GROUP_DIM)
    cs = jnp.cumsum(vf, axis=1)
    t = jnp.arange(S)
    pooled = []
    for g, w in enumerate(POOL_WINDOWS):
        c = cs[:, :, g]
        lag = jnp.pad(c, ((0, 0), (w, 0), (0, 0)))[:, :S]
        cnt = jnp.minimum(t + 1, w).astype(jnp.float32)[None, :, None]
        pooled.append((c - lag) / cnt)
    d = (jnp.stack(pooled, axis=2) - vf).astype(v.dtype)
    y = jnp.einsum('bsgc,gcd->bsgd', d, mix_w)
    return y.reshape(B_, S, POOL_WIDTH) * scale


def forgetting_attention(q, k, v, f_logit, f_bias):
    B_, S, _ = q.shape
    H, Dh = ATTN_HEADS, ATTN_HEAD_DIM
    q = q.reshape(B_, S, H, Dh).transpose(0, 2, 1, 3)
    k = k.reshape(B_, S, H, Dh).transpose(0, 2, 1, 3)
    v = v.reshape(B_, S, H, Dh).transpose(0, 2, 1, 3)
    logf = jax.nn.log_sigmoid((f_logit + f_bias).astype(jnp.float32))
    c = jnp.cumsum(logf, axis=1).transpose(0, 2, 1)
    nb = S // Q_BLOCK
    qb = q.reshape(B_, H, nb, Q_BLOCK, Dh).transpose(2, 0, 1, 3, 4)
    cb = c.reshape(B_, H, nb, Q_BLOCK).transpose(2, 0, 1, 3)
    kpos = jnp.arange(S)
    scale = Dh ** -0.5

    def block(args):
        qi, ci, i = args
        qpos = i * Q_BLOCK + jnp.arange(Q_BLOCK)
        s = jnp.einsum('bhqd,bhkd->bhqk', qi, k).astype(jnp.float32) * scale
        s = s + (ci[..., :, None] - c[:, :, None, :])
        s = jnp.where(kpos[None, :] <= qpos[:, None], s, -jnp.inf)
        p = jax.nn.softmax(s, axis=-1).astype(v.dtype)
        return jnp.einsum('bhqk,bhkd->bhqd', p, v)

    o = lax.map(block, (qb, cb, jnp.arange(nb)))
    return o.transpose(1, 0, 3, 2, 4).reshape(B_, S, H * Dh)


def segsum(a):
    L = a.shape[-1]
    cs = jnp.cumsum(a, axis=-1)
    seg = cs[..., :, None] - cs[..., None, :]
    mask = jnp.tril(jnp.ones((L, L), dtype=bool))
    return jnp.where(mask, seg, -jnp.inf)


def ssd_scan(x, dt, A, Bm, Cm):
    B_, S = x.shape[:2]
    L, G, N, P = SSD_CHUNK, SSD_GROUPS, SSD_STATE, SSD_HEAD_DIM
    E = SSD_HEADS // G
    nc = S // L
    xd = (x * dt[..., None]).reshape(B_, nc, L, G, E, P)
    a = (dt * A).reshape(B_, nc, L, G, E).transpose(0, 3, 4, 1, 2)
    Bc = Bm.reshape(B_, nc, L, G, N)
    Cc = Cm.reshape(B_, nc, L, G, N)
    a_cs = jnp.cumsum(a, axis=-1)
    Lmat = jnp.exp(segsum(a))
    cb = jnp.einsum('bclgn,bcsgn->bgcls', Cc, Bc)
    y_diag = jnp.einsum('bgcls,bgecls,bcsgep->bclgep', cb, Lmat, xd)
    decay = jnp.exp(a_cs[..., -1:] - a_cs)
    states = jnp.einsum('bclgn,bgecl,bclgep->cbgepn', Bc, decay, xd)
    chunk_decay = jnp.exp(a_cs[..., -1]).transpose(3, 0, 1, 2)

    def step(h, inp):
        s_c, d_c = inp
        return d_c[..., None, None] * h + s_c, h

    h0 = jnp.zeros(states.shape[1:], x.dtype)
    _, prev = lax.scan(step, h0, (states, chunk_decay))
    y_off = jnp.einsum('bclgn,cbgepn,bgecl->bclgep', Cc, prev, jnp.exp(a_cs))
    return (y_diag + y_off).reshape(B_, S, SSD_HEADS, P)


def ssd_mixer(z, xbc, dt_raw, conv_w, conv_b, dt_bias, a_log, d_skip, norm_w):
    B_, S, _ = z.shape
    f32 = jnp.float32
    xbc = jax.nn.silu(causal_dwconv(xbc, conv_w, conv_b))
    xs, Bm, Cm = jnp.split(xbc, [SSD_WIDTH, SSD_WIDTH + SSD_GROUPS * SSD_STATE], axis=-1)
    dt = jax.nn.softplus(dt_raw.astype(f32) + dt_bias.astype(f32))
    A = -jnp.exp(a_log.astype(f32))
    x = xs.astype(f32).reshape(B_, S, SSD_HEADS, SSD_HEAD_DIM)
    y = ssd_scan(x, dt, A,
                 Bm.astype(f32).reshape(B_, S, SSD_GROUPS, SSD_STATE),
                 Cm.astype(f32).reshape(B_, S, SSD_GROUPS, SSD_STATE))
    y = y + d_skip.astype(f32)[:, None] * x
    y = y.reshape(B_, S, SSD_WIDTH) * jax.nn.silu(z.astype(f32))
    yg = y.reshape(B_, S, SSD_GROUPS, SSD_WIDTH // SSD_GROUPS)
    yg = yg * lax.rsqrt(jnp.mean(yg * yg, axis=-1, keepdims=True) + NORM_EPS)
    return (yg.reshape(B_, S, SSD_WIDTH) * norm_w.astype(f32)).astype(z.dtype)


def conv_ffn(u, w_up, conv_w, conv_b, w_down):
    h = causal_dwconv(u @ w_up, conv_w, conv_b)
    g, val = jnp.split(h, 2, axis=-1)
    return (jax.nn.silu(g) * val) @ w_down


def setup_inputs(seed: int = 0) -> dict:
    key = jax.random.key(seed)
    ks = jax.random.split(key, 24)
    f32 = jnp.float32
    L = DEPTH

    def nrm(k, shape, scale):
        return jax.random.normal(k, shape, f32) * scale

    def gain(k, shape):
        return 1.0 + 0.02 * jax.random.normal(k, shape, f32)

    dt0 = jnp.exp(jax.random.uniform(ks[9], (L, SSD_HEADS), f32,
                                     math.log(1e-3), math.log(1e-1)))
    return {
        'x': jax.random.normal(ks[0], (BATCH, SEQ, D_MODEL), f32),
        'norm_mix': gain(ks[1], (L, D_MODEL)),
        'w_in': nrm(ks[2], (L, D_MODEL, IN_TOTAL), D_MODEL ** -0.5),
        'pool_mix': nrm(ks[3], (L, POOL_GROUPS, POOL_GROUP_DIM, POOL_GROUP_DIM), POOL_GROUP_DIM ** -0.5),
        'pool_scale': 1.0 + 0.1 * jax.random.normal(ks[4], (L, POOL_WIDTH), f32),
        'f_bias': jax.random.uniform(ks[5], (L, ATTN_HEADS), f32, 1.0, 4.0),
        'ssd_conv_w': nrm(ks[6], (L, SSD_CONV, SSD_CONV_CH), SSD_CONV ** -0.5),
        'ssd_conv_b': nrm(ks[7], (L, SSD_CONV_CH), 0.02),
        'ssd_dt_bias': dt0 + jnp.log(-jnp.expm1(-dt0)),
        'ssd_a_log': jnp.log(jax.random.uniform(ks[10], (L, SSD_HEADS), f32, 1.0, 16.0)),
        'ssd_d': 1.0 + 0.1 * jax.random.normal(ks[11], (L, SSD_HEADS), f32),
        'ssd_norm': gain(ks[12], (L, SSD_WIDTH)),
        'p_pool': nrm(ks[13], (L, POOL_WIDTH, D_MODEL), POOL_WIDTH ** -0.5),
        'p_attn': nrm(ks[14], (L, ATTN_WIDTH, D_MODEL), ATTN_WIDTH ** -0.5),
        'p_ssd': nrm(ks[15], (L, SSD_WIDTH, D_MODEL), SSD_WIDTH ** -0.5),
        'w_out': nrm(ks[16], (L, D_MODEL, D_MODEL), D_MODEL ** -0.5),
        'norm_ffn': gain(ks[17], (L, D_MODEL)),
        'ffn_up': nrm(ks[18], (L, D_MODEL, 2 * FFN_DIM), D_MODEL ** -0.5),
        'ffn_conv_w': nrm(ks[19], (L, FFN_CONV, 2 * FFN_DIM), FFN_CONV ** -0.5),
        'ffn_conv_b': nrm(ks[20], (L, 2 * FFN_DIM), 0.02),
        'ffn_down': nrm(ks[21], (L, FFN_DIM, D_MODEL), FFN_DIM ** -0.5),
        'norm_final': gain(ks[22], (D_MODEL,)),
    }


def reference(x, norm_mix, w_in, pool_mix, pool_scale, f_bias, ssd_conv_w, ssd_conv_b,
              ssd_dt_bias, ssd_a_log, ssd_d, ssd_norm, p_pool, p_attn, p_ssd, w_out,
              norm_ffn, ffn_up, ffn_conv_w, ffn_conv_b, ffn_down, norm_final):
    B_, S, D = x.shape
    for l in range(DEPTH):
        u = rmsnorm(x, norm_mix[l])
        (pool_v, q, k, v, f_logit, z, xbc, dt_raw, gate_logits) = split_columns(u @ w_in[l])
        y_pool = pool_mixer(pool_v, pool_mix[l], pool_scale[l]) @ p_pool[l]
        y_attn = forgetting_attention(q, k, v, f_logit, f_bias[l]) @ p_attn[l]
        y_ssd = ssd_mixer(z, xbc, dt_raw, ssd_conv_w[l], ssd_conv_b[l], ssd_dt_bias[l],
                          ssd_a_log[l], ssd_d[l], ssd_norm[l]) @ p_ssd[l]
        gates = jax.nn.sigmoid(gate_logits.astype(jnp.float32)).astype(x.dtype)
        gates = gates.reshape(B_, S, N_BRANCH, D)
        merged = gates[:, :, 0] * y_pool + gates[:, :, 1] * y_attn + gates[:, :, 2] * y_ssd
        x = x + merged @ w_out[l]
        x = x + conv_ffn(rmsnorm(x, norm_ffn[l]), ffn_up[l], ffn_conv_w[l], ffn_conv_b[l], ffn_down[l])
    return rmsnorm(x, norm_final)
```

```python
import functools

import jax
import jax.numpy as jnp
from jax import lax
from jax.experimental import pallas as pl
from jax.experimental.pallas import tpu as pltpu

F32 = jnp.float32
BF16 = jnp.bfloat16

D_MODEL = 1024
NORM_EPS = 1e-6
POOL_WINDOWS = (2, 4, 8, 16)
POOL_WIDTH = 512
ATTN_HEADS = 8
ATTN_HEAD_DIM = 64
ATTN_WIDTH = 512
SSD_HEADS = 16
SSD_HEAD_DIM = 64
SSD_WIDTH = 1024
SSD_GROUPS = 2
SSD_STATE = 128
SSD_CONV = 4
SSD_CHUNK = 128
SSD_CONV_CH = 1536
FFN_DIM = 2816
FFN_CONV = 3

LANES = 128
SUBLANES = 8

COL_GATES = 0
COL_Z = 3072
COL_POOL = 4096
COL_XBC = 4608
COL_Q = 6144
COL_K = 6656
COL_V = 7168
COL_AUX = 7680
PROJ_COLS = 8192
AUX_F0 = 0
AUX_DT0 = 8
AUXT_ROWS = 32

VMEM_LIMIT = 56 * 1024 * 1024


def _cparams(sem):
    return pltpu.CompilerParams(dimension_semantics=sem, vmem_limit_bytes=VMEM_LIMIT)


def _split3(x):
    hi = x.astype(BF16).astype(F32)
    r1 = x - hi
    mid = r1.astype(BF16).astype(F32)
    lo = (r1 - mid).astype(BF16).astype(F32)
    return hi, mid, lo


def _inproj_kernel(x_ref, nw_ref, w_ref, wauxt_ref, out_ref, auxt_ref, u_ref):
    @pl.when(pl.program_id(1) == 0)
    def _():
        x = x_ref[...]
        ms = jnp.mean(x * x, axis=-1, keepdims=True)
        u = (x * lax.rsqrt(ms + NORM_EPS) * nw_ref[...]).astype(BF16)
        u_ref[...] = u
        auxt_ref[...] = lax.dot_general(wauxt_ref[...], u, (((1,), (1,)), ((), ())),
                                        preferred_element_type=F32)

    out_ref[...] = jnp.dot(u_ref[...], w_ref[...], preferred_element_type=F32)


def _in_proj(x2d, norm_w, w_perm, wauxt, tm=1024, tn=1024):
    T = x2d.shape[0]
    return pl.pallas_call(
        _inproj_kernel,
        grid=(T // tm, PROJ_COLS // tn),
        in_specs=[
            pl.BlockSpec((tm, D_MODEL), lambda i, j: (i, 0)),
            pl.BlockSpec((1, D_MODEL), lambda i, j: (0, 0)),
            pl.BlockSpec((D_MODEL, tn), lambda i, j: (0, j)),
            pl.BlockSpec((AUXT_ROWS, D_MODEL), lambda i, j: (0, 0)),
        ],
        out_specs=[
            pl.BlockSpec((tm, tn), lambda i, j: (i, j)),
            pl.BlockSpec((AUXT_ROWS, tm), lambda i, j: (0, i)),
        ],
        out_shape=[
            jax.ShapeDtypeStruct((T, PROJ_COLS), F32),
            jax.ShapeDtypeStruct((AUXT_ROWS, T), F32),
        ],
        scratch_shapes=[pltpu.VMEM((tm, D_MODEL), BF16)],
        compiler_params=_cparams(("parallel", "arbitrary")),
        name="in_proj",
    )(x2d, norm_w, w_perm, wauxt)


AUG = 2 * ATTN_HEAD_DIM


def _foxprep_kernel(q_ref, k_ref, v_ref, aux_ref, fb_ref, pq_ref, pk_ref, ones_ref,
                    qa_ref, ka_ref, vb_ref, carry_ref):
    ts = q_ref.shape[0]

    @pl.when(pl.program_id(1) == 0)
    def _():
        carry_ref[...] = jnp.zeros_like(carry_ref)

    logf = jax.nn.log_sigmoid(aux_ref[...] + fb_ref[...])
    row = lax.broadcasted_iota(jnp.int32, (ts, ts), 0)
    col = lax.broadcasted_iota(jnp.int32, (ts, ts), 1)
    tril = (col <= row).astype(BF16)
    hi, mid, lo = _split3(logf)
    x3 = jnp.concatenate([hi, mid, lo], axis=1).astype(BF16)
    cs3 = jnp.dot(tril, x3, preferred_element_type=F32)
    c = cs3[:, :LANES] + cs3[:, LANES:2 * LANES] + cs3[:, 2 * LANES:] + carry_ref[...]
    carry_ref[...] = c[ts - 1:ts, :]

    ch, cm, cl = _split3(c)
    c3 = jnp.concatenate([ch, cm, cl], axis=1).astype(BF16)
    augq = jnp.dot(c3, pq_ref[...], preferred_element_type=F32) + ones_ref[0:1, :]
    augk = jnp.dot(c3, pk_ref[...], preferred_element_type=F32) + ones_ref[1:2, :]
    lane = lax.broadcasted_iota(jnp.int32, (ts, LANES), 1)
    is_q = lane < ATTN_HEAD_DIM
    scale = ATTN_HEAD_DIM ** -0.5
    for h in range(ATTN_HEADS):
        pair = slice((h // 2) * LANES, (h // 2 + 1) * LANES)
        blk = slice(h * LANES, (h + 1) * LANES)
        qp = q_ref[:, pair]
        kp = k_ref[:, pair]
        if h % 2 == 1:
            qp = pltpu.roll(qp, ATTN_HEAD_DIM, 1)
            kp = pltpu.roll(kp, ATTN_HEAD_DIM, 1)
        qa_ref[0, h] = jnp.where(is_q, qp * scale, augq[:, blk]).astype(BF16)
        ka_ref[0, h] = jnp.where(is_q, kp, augk[:, blk]).astype(BF16)
    vb_ref[...] = v_ref[...].astype(BF16)


def _fox_prep(proj, fb128, pq, pk, ones, B, S, ts=512):
    ns = S // ts
    T = B * S
    return pl.pallas_call(
        _foxprep_kernel,
        grid=(B, ns),
        in_specs=[
            pl.BlockSpec((ts, ATTN_WIDTH), lambda b, i: (b * ns + i, COL_Q // ATTN_WIDTH)),
            pl.BlockSpec((ts, ATTN_WIDTH), lambda b, i: (b * ns + i, COL_K // ATTN_WIDTH)),
            pl.BlockSpec((ts, ATTN_WIDTH), lambda b, i: (b * ns + i, COL_V // ATTN_WIDTH)),
            pl.BlockSpec((ts, LANES), lambda b, i: (b * ns + i, COL_AUX // LANES)),
            pl.BlockSpec((1, LANES), lambda b, i: (0, 0)),
            pl.BlockSpec((3 * LANES, ATTN_HEADS * AUG), lambda b, i: (0, 0)),
            pl.BlockSpec((3 * LANES, ATTN_HEADS * AUG), lambda b, i: (0, 0)),
            pl.BlockSpec((SUBLANES, ATTN_HEADS * AUG), lambda b, i: (0, 0)),
        ],
        out_specs=[
            pl.BlockSpec((1, ATTN_HEADS, ts, AUG), lambda b, i: (b, 0, i, 0)),
            pl.BlockSpec((1, ATTN_HEADS, ts, AUG), lambda b, i: (b, 0, i, 0)),
            pl.BlockSpec((ts, ATTN_WIDTH), lambda b, i: (b * ns + i, 0)),
        ],
        out_shape=[
            jax.ShapeDtypeStruct((B, ATTN_HEADS, S, AUG), BF16),
            jax.ShapeDtypeStruct((B, ATTN_HEADS, S, AUG), BF16),
            jax.ShapeDtypeStruct((T, ATTN_WIDTH), BF16),
        ],
        scratch_shapes=[pltpu.VMEM((1, LANES), F32)],
        compiler_params=_cparams(("parallel", "arbitrary")),
        name="fox_prep",
    )(proj, proj, proj, proj, fb128, pq, pk, ones)


def _fox_constants():
    rows = jnp.arange(3 * LANES)
    part = rows // LANES
    head = rows % LANES
    cols = jnp.arange(ATTN_HEADS * AUG)
    chead = cols // AUG
    clane = cols % AUG
    valid = (head[:, None] == chead[None, :]) & (head[:, None] < ATTN_HEADS)
    pq = jnp.where(valid & (clane[None, :] == ATTN_HEAD_DIM + part[:, None]), 1.0, 0.0)
    pk = jnp.where(valid & (clane[None, :] == ATTN_HEAD_DIM + 3 + part[:, None]), -1.0, 0.0)
    ones_q = ((clane >= ATTN_HEAD_DIM + 3) & (clane < ATTN_HEAD_DIM + 6)).astype(F32)
    ones_k = ((clane >= ATTN_HEAD_DIM) & (clane < ATTN_HEAD_DIM + 3)).astype(F32)
    ones = jnp.zeros((SUBLANES, ATTN_HEADS * AUG), F32).at[0].set(ones_q).at[1].set(ones_k)
    return pq.astype(BF16), pk.astype(BF16), ones


def _flash_kernel(q_ref, k_ref, v_ref, o_ref, m_ref, l_ref, acc_ref):
    i = pl.program_id(2)
    j = pl.program_id(3)
    tq = q_ref.shape[2]
    tk = k_ref.shape[2]

    @pl.when(j == 0)
    def _():
        m_ref[...] = jnp.full_like(m_ref, -jnp.inf)
        l_ref[...] = jnp.zeros_like(l_ref)
        acc_ref[...] = jnp.zeros_like(acc_ref)

    def step(masked):
        v = v_ref[...]
        for h in range(2):
            s = lax.dot_general(q_ref[0, h], k_ref[0, h], (((1,), (1,)), ((), ())),
                                preferred_element_type=F32)
            if masked:
                row = lax.broadcasted_iota(jnp.int32, (tq, tk), 0)
                col = lax.broadcasted_iota(jnp.int32, (tq, tk), 1)
                s = jnp.where(col <= row, s, -jnp.inf)
            m_prev = m_ref[h]
            m_next = jnp.maximum(m_prev, jnp.max(s, axis=1, keepdims=True))
            alpha = jnp.exp(m_prev - m_next)
            p = jnp.exp(s - m_next[:, 0:1])
            l_ref[h] = alpha * l_ref[h] + jnp.sum(p, axis=1, keepdims=True)
            acc_ref[h] = alpha * acc_ref[h] + jnp.dot(p.astype(BF16), v,
                                                      preferred_element_type=F32)
            m_ref[h] = m_next

    @pl.when(j < i)
    def _():
        step(False)

    @pl.when(j == i)
    def _():
        step(True)
        lane = lax.broadcasted_iota(jnp.int32, (tq, LANES), 1)
        o = jnp.where(lane < ATTN_HEAD_DIM, acc_ref[0] / l_ref[0], acc_ref[1] / l_ref[1])
        o_ref[...] = o.astype(o_ref.dtype)


def _flash(qa, ka, vb, B, S, t=512):
    n = S // t
    T = B * S
    hp = ATTN_HEADS // 2
    return pl.pallas_call(
        _flash_kernel,
        grid=(B, hp, n, n),
        in_specs=[
            pl.BlockSpec((1, 2, t, AUG), lambda b, h, i, j: (b, h, i, 0)),
            pl.BlockSpec((1, 2, t, AUG), lambda b, h, i, j: (b, h, jnp.minimum(j, i), 0)),
            pl.BlockSpec((t, LANES), lambda b, h, i, j: (b * n + jnp.minimum(j, i), h)),
        ],
        out_specs=pl.BlockSpec((t, LANES), lambda b, h, i, j: (b * n + i, h)),
        out_shape=jax.ShapeDtypeStruct((T, ATTN_WIDTH), BF16),
        scratch_shapes=[
            pltpu.VMEM((2, t, LANES), F32),
            pltpu.VMEM((2, t, LANES), F32),
            pltpu.VMEM((2, t, LANES), F32),
        ],
        compiler_params=_cparams(("parallel", "parallel", "parallel", "arbitrary")),
        name="flash",
    )(qa, ka, vb)


PACK_STRIDE = 32


def _pack3(x, lane):
    hi, mid, lo = _split3(x)
    packed = jnp.where(lane < PACK_STRIDE, hi,
                       jnp.where(lane < 2 * PACK_STRIDE, pltpu.roll(mid, PACK_STRIDE, 1),
                                 pltpu.roll(lo, 2 * PACK_STRIDE, 1)))
    return packed.astype(BF16)


def _ssd_kernel(z_ref, xbc_ref, aux_ref, auxt_ref, cw_ref, cb_ref, dtb_ref, alog_ref,
                dtbt_ref, alogt_ref, dexp_ref, nw_ref, xp_ref,
                y_ref, xext_ref, state_ref):
    ts = z_ref.shape[0]
    L = SSD_CHUNK
    halo = SUBLANES

    @pl.when(pl.program_id(1) == 0)
    def _():
        state_ref[...] = jnp.zeros_like(state_ref)
        xext_ref[0:halo, :] = jnp.zeros((halo, SSD_CONV_CH), F32)

    xext_ref[halo:halo + ts, :] = xbc_ref[...]
    conv = cb_ref[...] + cw_ref[SSD_CONV - 1:SSD_CONV, :] * xext_ref[halo:halo + ts, :]
    for k in range(SSD_CONV - 1):
        off = halo - (SSD_CONV - 1) + k
        conv = conv + cw_ref[k:k + 1, :] * xext_ref[off:off + ts, :]
    xext_ref[0:halo, :] = xext_ref[ts:ts + halo, :]
    xc = jax.nn.silu(conv)

    dt = jax.nn.softplus(aux_ref[...] + dtb_ref[...])
    a = dt * (-jnp.exp(alog_ref[...]))
    dtt = jax.nn.softplus(auxt_ref[...] + dtbt_ref[...])
    at = dtt * (-jnp.exp(alogt_ref[...]))

    row = lax.broadcasted_iota(jnp.int32, (L, L), 0)
    col = lax.broadcasted_iota(jnp.int32, (L, L), 1)
    causal = col <= row
    tril = causal.astype(BF16)
    triu = (row <= col).astype(BF16)
    lane = lax.broadcasted_iota(jnp.int32, (L, LANES), 1)
    first_half = lane < SSD_HEAD_DIM
    xp = xp_ref[...]
    gw = SSD_WIDTH // SSD_GROUPS
    hpg = SSD_HEADS // SSD_GROUPS

    for c in range(ts // L):
        r = slice(c * L, (c + 1) * L)
        a_c = a[r]
        ah, am, al = _split3(a_c)
        a3 = jnp.concatenate([ah, am, al], axis=1).astype(BF16)
        cs3 = jnp.dot(tril, a3, preferred_element_type=F32)
        a_cs = cs3[:, :LANES] + cs3[:, LANES:2 * LANES] + cs3[:, 2 * LANES:]
        th, tm_, tl = _split3(at[:, r])
        t3 = jnp.concatenate([th, tm_, tl], axis=0).astype(BF16)
        cst = jnp.dot(t3, triu, preferred_element_type=F32)
        a_cst = cst[0:AUXT_ROWS] + cst[AUXT_ROWS:2 * AUXT_ROWS] + cst[2 * AUXT_ROWS:]

        e_small = jnp.exp(a_cs)
        w_small = jnp.exp(a_cs[L - 1:L, :] - a_cs) * dt[r]
        dt_exp = jnp.dot(_pack3(dt[r], lane), xp, preferred_element_type=F32)
        w_exp = jnp.dot(_pack3(w_small, lane), xp, preferred_element_type=F32)
        e_exp = jnp.dot(_pack3(e_small, lane), xp, preferred_element_type=F32)

        x_c = xc[r, 0:SSD_WIDTH]
        xd = (x_c * dt_exp).astype(BF16)
        xw = (x_c * w_exp).astype(BF16)
        y_parts = []
        for g in range(SSD_GROUPS):
            b_g = xc[r, SSD_WIDTH + g * SSD_STATE:SSD_WIDTH + (g + 1) * SSD_STATE]
            c_g = xc[r, SSD_WIDTH + (SSD_GROUPS + g) * SSD_STATE:
                     SSD_WIDTH + (SSD_GROUPS + g + 1) * SSD_STATE]
            c_gb = c_g.astype(BF16)
            cb = lax.dot_general(c_gb, b_g.astype(BF16), (((1,), (1,)), ((), ())),
                                 preferred_element_type=F32)
            gcols = slice(g * gw, (g + 1) * gw)
            st = state_ref[:, gcols]
            y_off = jnp.dot(c_gb, st.astype(BF16), preferred_element_type=F32) * e_exp[:, gcols]
            state_ref[:, gcols] = e_exp[L - 1:L, gcols] * st + jnp.dot(
                b_g.T.astype(BF16), xw[:, gcols], preferred_element_type=F32)
            for pr in range(hpg // 2):
                ms = []
                for e in range(2):
                    h = g * hpg + pr * 2 + e
                    seg = a_cs[:, AUX_DT0 + h:AUX_DT0 + h + 1] - a_cst[AUX_DT0 + h:AUX_DT0 + h + 1, :]
                    lm = jnp.exp(jnp.where(causal, seg, -jnp.inf))
                    ms.append((cb * lm).astype(BF16))
                lhs = jnp.concatenate(ms, axis=1)
                p0 = g * gw + pr * LANES
                xpair = xd[:, p0:p0 + LANES]
                zero = jnp.zeros_like(xpair)
                rhs = jnp.concatenate([jnp.where(first_half, xpair, zero),
                                       jnp.where(first_half, zero, xpair)], axis=0)
                y_parts.append(jnp.dot(lhs, rhs, preferred_element_type=F32)
                               + y_off[:, pr * LANES:(pr + 1) * LANES])
        y = jnp.concatenate(y_parts, axis=1) + dexp_ref[...] * x_c
        y = y * jax.nn.silu(z_ref[r, :])
        outs = []
        for g in range(SSD_GROUPS):
            yg = y[:, g * gw:(g + 1) * gw]
            yg = yg * lax.rsqrt(jnp.mean(yg * yg, axis=-1, keepdims=True) + NORM_EPS)
            outs.append(yg)
        y_ref[r, :] = (jnp.concatenate(outs, axis=1) * nw_ref[...]).astype(y_ref.dtype)


def _ssd(proj, auxt, cw, cb, dtb128, alog128, dtbt, alogt, dexp, nw, xp, B, S, ts=256):
    ns = S // ts
    T = B * S
    return pl.pallas_call(
        _ssd_kernel,
        grid=(B, ns),
        in_specs=[
            pl.BlockSpec((ts, SSD_WIDTH), lambda b, i: (b * ns + i, COL_Z // SSD_WIDTH)),
            pl.BlockSpec((ts, SSD_CONV_CH), lambda b, i: (b * ns + i, COL_XBC // SSD_CONV_CH)),
            pl.BlockSpec((ts, LANES), lambda b, i: (b * ns + i, COL_AUX // LANES)),
            pl.BlockSpec((AUXT_ROWS, ts), lambda b, i: (0, b * ns + i)),
            pl.BlockSpec((SSD_CONV, SSD_CONV_CH), lambda b, i: (0, 0)),
            pl.BlockSpec((1, SSD_CONV_CH), lambda b, i: (0, 0)),
            pl.BlockSpec((1, LANES), lambda b, i: (0, 0)),
            pl.BlockSpec((1, LANES), lambda b, i: (0, 0)),
            pl.BlockSpec((AUXT_ROWS, 1), lambda b, i: (0, 0)),
            pl.BlockSpec((AUXT_ROWS, 1), lambda b, i: (0, 0)),
            pl.BlockSpec((1, SSD_WIDTH), lambda b, i: (0, 0)),
            pl.BlockSpec((1, SSD_WIDTH), lambda b, i: (0, 0)),
            pl.BlockSpec((LANES, SSD_WIDTH), lambda b, i: (0, 0)),
        ],
        out_specs=pl.BlockSpec((ts, SSD_WIDTH), lambda b, i: (b * ns + i, 0)),
        out_shape=jax.ShapeDtypeStruct((T, SSD_WIDTH), BF16),
        scratch_shapes=[
            pltpu.VMEM((ts + SUBLANES, SSD_CONV_CH), F32),
            pltpu.VMEM((SSD_STATE, SSD_WIDTH), F32),
        ],
        compiler_params=_cparams(("parallel", "arbitrary")),
        name="ssd",
    )(proj, proj, proj, auxt, cw, cb, dtb128, alog128, dtbt, alogt, dexp, nw, xp)


def _ssd_expand_matrix():
    rows = jnp.arange(LANES)
    slot = rows % PACK_STRIDE
    part = rows // PACK_STRIDE
    head = slot - AUX_DT0
    cols = jnp.arange(SSD_WIDTH) // SSD_HEAD_DIM
    valid = (part < 3) & (head >= 0) & (head < SSD_HEADS)
    return jnp.where(valid[:, None] & (head[:, None] == cols[None, :]), 1.0, 0.0).astype(BF16)


POOL_HALO = 16


def _merge_kernel(x_ref, gates_ref, pool_ref, prev_ref, attn_ref, ssd_ref, mixw_ref, pscale_ref,
                  ppool_ref, pattn_ref, pssd_ref, wout_ref, o_ref, ext_ref):
    tm = x_ref.shape[0]
    i = pl.program_id(1)
    gdim = POOL_WIDTH // len(POOL_WINDOWS)

    prev = prev_ref[...]
    ext_ref[0:POOL_HALO, :] = jnp.where(i == 0, jnp.zeros_like(prev), prev)
    ext_ref[POOL_HALO:POOL_HALO + tm, :] = pool_ref[...]
    pos = i * tm + lax.broadcasted_iota(jnp.int32, (tm, gdim), 0)
    ys = []
    for g, w in enumerate(POOL_WINDOWS):
        cols = slice(g * gdim, (g + 1) * gdim)
        v = ext_ref[POOL_HALO:POOL_HALO + tm, cols]
        acc = v
        for k in range(1, w):
            acc = acc + ext_ref[POOL_HALO - k:POOL_HALO - k + tm, cols]
        cnt = jnp.minimum(pos + 1, w).astype(F32)
        d = (acc / cnt - v).astype(BF16)
        ys.append(jnp.dot(d, mixw_ref[g], preferred_element_type=F32))
    ypre = (jnp.concatenate(ys, axis=1) * pscale_ref[...]).astype(BF16)
    y_pool = jnp.dot(ypre, ppool_ref[...], preferred_element_type=F32)
    y_attn = jnp.dot(attn_ref[...], pattn_ref[...], preferred_element_type=F32)
    y_ssd = jnp.dot(ssd_ref[...], pssd_ref[...], preferred_element_type=F32)
    merged = (jax.nn.sigmoid(gates_ref[:, 0:D_MODEL]) * y_pool
              + jax.nn.sigmoid(gates_ref[:, D_MODEL:2 * D_MODEL]) * y_attn
              + jax.nn.sigmoid(gates_ref[:, 2 * D_MODEL:3 * D_MODEL]) * y_ssd)
    o_ref[...] = x_ref[...] + jnp.dot(merged.astype(BF16), wout_ref[...],
                                      preferred_element_type=F32)


def _merge(x2d, proj, attn, yssd, mixw, pscale, ppool, pattn, pssd, wout, B, S, tm=256):
    ns = S // tm
    T = B * S
    hb = tm // POOL_HALO
    const2 = lambda b, i: (0, 0)
    return pl.pallas_call(
        _merge_kernel,
        grid=(B, ns),
        in_specs=[
            pl.BlockSpec((tm, D_MODEL), lambda b, i: (b * ns + i, 0)),
            pl.BlockSpec((tm, 3 * D_MODEL), lambda b, i: (b * ns + i, 0)),
            pl.BlockSpec((tm, POOL_WIDTH), lambda b, i: (b * ns + i, COL_POOL // POOL_WIDTH)),
            pl.BlockSpec((POOL_HALO, POOL_WIDTH),
                         lambda b, i: (jnp.maximum((b * ns + i) * hb - 1, 0), COL_POOL // POOL_WIDTH)),
            pl.BlockSpec((tm, ATTN_WIDTH), lambda b, i: (b * ns + i, 0)),
            pl.BlockSpec((tm, SSD_WIDTH), lambda b, i: (b * ns + i, 0)),
            pl.BlockSpec((len(POOL_WINDOWS), LANES, LANES), lambda b, i: (0, 0, 0)),
            pl.BlockSpec((1, POOL_WIDTH), const2),
            pl.BlockSpec((POOL_WIDTH, D_MODEL), const2),
            pl.BlockSpec((ATTN_WIDTH, D_MODEL), const2),
            pl.BlockSpec((SSD_WIDTH, D_MODEL), const2),
            pl.BlockSpec((D_MODEL, D_MODEL), const2),
        ],
        out_specs=pl.BlockSpec((tm, D_MODEL), lambda b, i: (b * ns + i, 0)),
        out_shape=jax.ShapeDtypeStruct((T, D_MODEL), F32),
        scratch_shapes=[pltpu.VMEM((tm + POOL_HALO, POOL_WIDTH), F32)],
        compiler_params=_cparams(("parallel", "parallel")),
        name="merge",
    )(x2d, proj, proj, proj, attn, yssd, mixw, pscale, ppool, pattn, pssd, wout)


def _ffn_kernel(x_ref, nw_ref, wg_ref, wv_ref, cwg_ref, cwv_ref, cbg_ref, cbv_ref, wd_ref, fnw_ref,
                o_ref, u_ref, acc_ref, eg_ref, ev_ref, carg_ref, carv_ref, *, final_norm):
    tm = x_ref.shape[0]
    i = pl.program_id(1)
    j = pl.program_id(2)
    nf = pl.num_programs(2)
    halo = SUBLANES

    @pl.when(j == 0)
    def _():
        x = x_ref[...]
        ms = jnp.mean(x * x, axis=-1, keepdims=True)
        u_ref[...] = (x * lax.rsqrt(ms + NORM_EPS) * nw_ref[...]).astype(BF16)
        acc_ref[...] = jnp.zeros_like(acc_ref)

    @pl.when(i == 0)
    def _():
        carg_ref[j] = jnp.zeros(carg_ref.shape[1:], F32)
        carv_ref[j] = jnp.zeros(carv_ref.shape[1:], F32)

    u = u_ref[...]

    def conv_half(w_ref, ext_ref, car_ref, cw_ref, cb_ref):
        h = jnp.dot(u, w_ref[...], preferred_element_type=F32)
        ext_ref[0:halo, :] = car_ref[j]
        ext_ref[halo:halo + tm, :] = h
        car_ref[j] = h[tm - halo:tm, :]
        out = cb_ref[...] + cw_ref[FFN_CONV - 1:FFN_CONV, :] * h
        for k in range(FFN_CONV - 1):
            off = halo - (FFN_CONV - 1) + k
            out = out + cw_ref[k:k + 1, :] * ext_ref[off:off + tm, :]
        return out

    g = conv_half(wg_ref, eg_ref, carg_ref, cwg_ref, cbg_ref)
    v = conv_half(wv_ref, ev_ref, carv_ref, cwv_ref, cbv_ref)
    act = (jax.nn.silu(g) * v).astype(BF16)
    acc_ref[...] += jnp.dot(act, wd_ref[...], preferred_element_type=F32)

    @pl.when(j == nf - 1)
    def _():
        y = x_ref[...] + acc_ref[...]
        if final_norm:
            ms = jnp.mean(y * y, axis=-1, keepdims=True)
            y = y * lax.rsqrt(ms + NORM_EPS) * fnw_ref[...]
        o_ref[...] = y


def _ffn(x2d, nw, wup, cw, cb, wdown, fnw, B, S, final_norm, tm=512, tf=1408):
    ns = S // tm
    nf = FFN_DIM // tf
    T = B * S
    return pl.pallas_call(
        functools.partial(_ffn_kernel, final_norm=final_norm),
        grid=(B, ns, nf),
        in_specs=[
            pl.BlockSpec((tm, D_MODEL), lambda b, i, j: (b * ns + i, 0)),
            pl.BlockSpec((1, D_MODEL), lambda b, i, j: (0, 0)),
            pl.BlockSpec((D_MODEL, tf), lambda b, i, j: (0, j)),
            pl.BlockSpec((D_MODEL, tf), lambda b, i, j: (0, nf + j)),
            pl.BlockSpec((FFN_CONV, tf), lambda b, i, j: (0, j)),
            pl.BlockSpec((FFN_CONV, tf), lambda b, i, j: (0, nf + j)),
            pl.BlockSpec((1, tf), lambda b, i, j: (0, j)),
            pl.BlockSpec((1, tf), lambda b, i, j: (0, nf + j)),
            pl.BlockSpec((tf, D_MODEL), lambda b, i, j: (j, 0)),
            pl.BlockSpec((1, D_MODEL), lambda b, i, j: (0, 0)),
        ],
        out_specs=pl.BlockSpec((tm, D_MODEL), lambda b, i, j: (b * ns + i, 0)),
        out_shape=jax.ShapeDtypeStruct((T, D_MODEL), F32),
        scratch_shapes=[
            pltpu.VMEM((tm, D_MODEL), BF16),
            pltpu.VMEM((tm, D_MODEL), F32),
            pltpu.VMEM((tm + SUBLANES, tf), F32),
            pltpu.VMEM((tm + SUBLANES, tf), F32),
            pltpu.VMEM((nf, SUBLANES, tf), F32),
            pltpu.VMEM((nf, SUBLANES, tf), F32),
        ],
        compiler_params=_cparams(("parallel", "arbitrary", "arbitrary")),
        name="ffn",
    )(x2d, nw, wup, wup, cw, cw, cb, cb, wdown, fnw)


def _pad_lanes(v, offset, width=LANES):
    return jnp.zeros((1, width), F32).at[0, offset:offset + v.shape[0]].set(v.astype(F32))


def _permute_w_in(w):
    o = 0
    segs = {}
    for name, n in (("pool", 512), ("q", 512), ("k", 512), ("v", 512), ("f", ATTN_HEADS),
                    ("z", 1024), ("xbc", SSD_CONV_CH), ("dt", SSD_HEADS), ("gates", 3 * D_MODEL)):
        segs[name] = w[:, o:o + n]
        o += n
    small = jnp.concatenate([segs["f"], segs["dt"]], axis=1)
    aux = jnp.pad(small, ((0, 0), (0, PROJ_COLS - COL_AUX - small.shape[1])))
    w_perm = jnp.concatenate([segs["gates"], segs["z"], segs["pool"], segs["xbc"],
                              segs["q"], segs["k"], segs["v"], aux], axis=1).astype(BF16)
    wauxt = jnp.pad(small, ((0, 0), (0, AUXT_ROWS - small.shape[1]))).T.astype(BF16)
    return w_perm, wauxt


def kernel(x, norm_mix, w_in, pool_mix, pool_scale, f_bias, ssd_conv_w, ssd_conv_b, ssd_dt_bias,
           ssd_a_log, ssd_d, ssd_norm, p_pool, p_attn, p_ssd, w_out, norm_ffn, ffn_up, ffn_conv_w,
           ffn_conv_b, ffn_down, norm_final):
    B, S, D = x.shape
    depth = w_in.shape[0]
    T = B * S
    x2d = x.reshape(T, D)
    pq, pk, ones = _fox_constants()
    xp = _ssd_expand_matrix()
    fnw = norm_final.reshape(1, D)
    for l in range(depth):
        w_perm, wauxt = _permute_w_in(w_in[l])
        proj, auxt = _in_proj(x2d, norm_mix[l].reshape(1, D), w_perm, wauxt)
        qa, ka, vb = _fox_prep(proj, _pad_lanes(f_bias[l], AUX_F0), pq, pk, ones, B, S)
        attn = _flash(qa, ka, vb, B, S)
        dtb128 = _pad_lanes(ssd_dt_bias[l], AUX_DT0)
        alog128 = _pad_lanes(ssd_a_log[l], AUX_DT0)
        yssd = _ssd(proj, auxt, ssd_conv_w[l], ssd_conv_b[l].reshape(1, -1), dtb128, alog128,
                    _pad_lanes(ssd_dt_bias[l], AUX_DT0, AUXT_ROWS).T,
                    _pad_lanes(ssd_a_log[l], AUX_DT0, AUXT_ROWS).T,
                    jnp.repeat(ssd_d[l], SSD_HEAD_DIM).reshape(1, -1), ssd_norm[l].reshape(1, -1),
                    xp, B, S)
        x2d = _merge(x2d, proj, attn, yssd, pool_mix[l].astype(BF16), pool_scale[l].reshape(1, -1),
                     p_pool[l].astype(BF16), p_attn[l].astype(BF16), p_ssd[l].astype(BF16),
                     w_out[l].astype(BF16), B, S)
        x2d = _ffn(x2d, norm_ffn[l].reshape(1, D), ffn_up[l].astype(BF16), ffn_conv_w[l],
                   ffn_conv_b[l].reshape(1, -1), ffn_down[l].astype(BF16), fnw, B, S,
                   final_norm=(l == depth - 1))
    return x2d.reshape(B, S, D)
```

```python
import functools

import jax
import jax.numpy as jnp
from jax import lax
from jax.experimental import pallas as pl
from jax.experimental.pallas import tpu as pltpu

F32 = jnp.float32
BF16 = jnp.bfloat16

D_MODEL = 1024
NORM_EPS = 1e-6
POOL_WINDOWS = (2, 4, 8, 16)
POOL_WIDTH = 512
ATTN_HEADS = 8
ATTN_HEAD_DIM = 64
ATTN_WIDTH = 512
SSD_HEADS = 16
SSD_HEAD_DIM = 64
SSD_WIDTH = 1024
SSD_GROUPS = 2
SSD_STATE = 128
SSD_CONV = 4
SSD_CHUNK = 128
SSD_CONV_CH = 1536
FFN_DIM = 2816
FFN_CONV = 3

LANES = 128
SUBLANES = 8

COL_GATES = 0
COL_Z = 3072
COL_POOL = 4096
COL_XBC = 4608
COL_Q = 6144
COL_K = 6656
COL_V = 7168
COL_AUX = 7680
PROJ_COLS = 8192
AUX_F0 = 0
AUX_DT0 = 8
AUXT_ROWS = 32

VMEM_LIMIT = 56 * 1024 * 1024


def _cparams(sem):
    return pltpu.CompilerParams(dimension_semantics=sem, vmem_limit_bytes=VMEM_LIMIT)


def _split3(x):
    hi = x.astype(BF16).astype(F32)
    r1 = x - hi
    mid = r1.astype(BF16).astype(F32)
    lo = (r1 - mid).astype(BF16).astype(F32)
    return hi, mid, lo


def _inproj_kernel(x_ref, nw_ref, w_ref, wauxt_ref, out_ref, auxt_ref, u_ref):
    @pl.when(pl.program_id(1) == 0)
    def _():
        x = x_ref[...]
        ms = jnp.mean(x * x, axis=-1, keepdims=True)
        u = (x * lax.rsqrt(ms + NORM_EPS) * nw_ref[...]).astype(BF16)
        u_ref[...] = u
        auxt_ref[...] = lax.dot_general(wauxt_ref[...], u, (((1,), (1,)), ((), ())),
                                        preferred_element_type=F32)

    out_ref[...] = jnp.dot(u_ref[...], w_ref[...], preferred_element_type=F32)


def _in_proj(x2d, norm_w, w_perm, wauxt, tm=1024, tn=1024):
    T = x2d.shape[0]
    return pl.pallas_call(
        _inproj_kernel,
        grid=(T // tm, PROJ_COLS // tn),
        in_specs=[
            pl.BlockSpec((tm, D_MODEL), lambda i, j: (i, 0)),
            pl.BlockSpec((1, D_MODEL), lambda i, j: (0, 0)),
            pl.BlockSpec((D_MODEL, tn), lambda i, j: (0, j)),
            pl.BlockSpec((AUXT_ROWS, D_MODEL), lambda i, j: (0, 0)),
        ],
        out_specs=[
            pl.BlockSpec((tm, tn), lambda i, j: (i, j)),
            pl.BlockSpec((AUXT_ROWS, tm), lambda i, j: (0, i)),
        ],
        out_shape=[
            jax.ShapeDtypeStruct((T, PROJ_COLS), F32),
            jax.ShapeDtypeStruct((AUXT_ROWS, T), F32),
        ],
        scratch_shapes=[pltpu.VMEM((tm, D_MODEL), BF16)],
        compiler_params=_cparams(("parallel", "arbitrary")),
        name="in_proj",
    )(x2d, norm_w, w_perm, wauxt)


AUG = 2 * ATTN_HEAD_DIM
ATTN_TILE = 512


def _foxprep_kernel(q_ref, k_ref, v_ref, aux_ref, fb_ref, pq_ref, pk_ref, ones_ref,
                    qt_ref, ka_ref, vt_ref, carry_ref):
    ts = q_ref.shape[0]

    @pl.when(pl.program_id(1) == 0)
    def _():
        carry_ref[...] = jnp.zeros_like(carry_ref)

    logf = jax.nn.log_sigmoid(aux_ref[...] + fb_ref[...])
    row = lax.broadcasted_iota(jnp.int32, (ts, ts), 0)
    col = lax.broadcasted_iota(jnp.int32, (ts, ts), 1)
    tril = (col <= row).astype(BF16)
    hi, mid, lo = _split3(logf)
    x3 = jnp.concatenate([hi, mid, lo], axis=1).astype(BF16)
    cs3 = jnp.dot(tril, x3, preferred_element_type=F32)
    c = cs3[:, :LANES] + cs3[:, LANES:2 * LANES] + cs3[:, 2 * LANES:] + carry_ref[...]
    carry_ref[...] = c[ts - 1:ts, :]

    ch, cm, cl = _split3(c)
    c3 = jnp.concatenate([ch, cm, cl], axis=1).astype(BF16)
    augq = jnp.dot(c3, pq_ref[...], preferred_element_type=F32) + ones_ref[0:1, :]
    augk = jnp.dot(c3, pk_ref[...], preferred_element_type=F32) + ones_ref[1:2, :]
    lane = lax.broadcasted_iota(jnp.int32, (ts, LANES), 1)
    is_q = lane < ATTN_HEAD_DIM
    scale = ATTN_HEAD_DIM ** -0.5
    row_id = lax.broadcasted_iota(jnp.int32, (ATTN_HEAD_DIM, ts), 0)
    ones_rows = jnp.where(row_id == 0, 1.0, 0.0)
    for h in range(ATTN_HEADS):
        pair = slice((h // 2) * LANES, (h // 2 + 1) * LANES)
        blk = slice(h * LANES, (h + 1) * LANES)
        qp = q_ref[:, pair]
        kp = k_ref[:, pair]
        if h % 2 == 1:
            qp = pltpu.roll(qp, ATTN_HEAD_DIM, 1)
            kp = pltpu.roll(kp, ATTN_HEAD_DIM, 1)
        qa = jnp.where(is_q, qp * scale, augq[:, blk])
        qt_ref[0, h] = qa.T.astype(BF16)
        ka_ref[0, h] = jnp.where(is_q, kp, augk[:, blk]).astype(BF16)
        vt = v_ref[:, pair].T
        vh = vt[(h % 2) * ATTN_HEAD_DIM:(h % 2 + 1) * ATTN_HEAD_DIM, :]
        vt_ref[0, h, 0] = jnp.concatenate([vh, ones_rows], axis=0).astype(BF16)


def _fox_prep(proj, fb128, pq, pk, ones, B, S, ts):
    ns = S // ts
    return pl.pallas_call(
        _foxprep_kernel,
        grid=(B, ns),
        in_specs=[
            pl.BlockSpec((ts, ATTN_WIDTH), lambda b, i: (b * ns + i, COL_Q // ATTN_WIDTH)),
            pl.BlockSpec((ts, ATTN_WIDTH), lambda b, i: (b * ns + i, COL_K // ATTN_WIDTH)),
            pl.BlockSpec((ts, ATTN_WIDTH), lambda b, i: (b * ns + i, COL_V // ATTN_WIDTH)),
            pl.BlockSpec((ts, LANES), lambda b, i: (b * ns + i, COL_AUX // LANES)),
            pl.BlockSpec((1, LANES), lambda b, i: (0, 0)),
            pl.BlockSpec((3 * LANES, ATTN_HEADS * AUG), lambda b, i: (0, 0)),
            pl.BlockSpec((3 * LANES, ATTN_HEADS * AUG), lambda b, i: (0, 0)),
            pl.BlockSpec((SUBLANES, ATTN_HEADS * AUG), lambda b, i: (0, 0)),
        ],
        out_specs=[
            pl.BlockSpec((1, ATTN_HEADS, AUG, ts), lambda b, i: (b, 0, 0, i)),
            pl.BlockSpec((1, ATTN_HEADS, ts, AUG), lambda b, i: (b, 0, i, 0)),
            pl.BlockSpec((1, ATTN_HEADS, 1, AUG, ts), lambda b, i: (b, 0, i, 0, 0)),
        ],
        out_shape=[
            jax.ShapeDtypeStruct((B, ATTN_HEADS, AUG, S), BF16),
            jax.ShapeDtypeStruct((B, ATTN_HEADS, S, AUG), BF16),
            jax.ShapeDtypeStruct((B, ATTN_HEADS, ns, AUG, ts), BF16),
        ],
        scratch_shapes=[pltpu.VMEM((1, LANES), F32)],
        compiler_params=_cparams(("parallel", "arbitrary")),
        name="fox_prep",
    )(proj, proj, proj, proj, fb128, pq, pk, ones)


def _fox_constants():
    rows = jnp.arange(3 * LANES)
    part = rows // LANES
    head = rows % LANES
    cols = jnp.arange(ATTN_HEADS * AUG)
    chead = cols // AUG
    clane = cols % AUG
    valid = (head[:, None] == chead[None, :]) & (head[:, None] < ATTN_HEADS)
    pq = jnp.where(valid & (clane[None, :] == ATTN_HEAD_DIM + part[:, None]), 1.0, 0.0)
    pk = jnp.where(valid & (clane[None, :] == ATTN_HEAD_DIM + 3 + part[:, None]), -1.0, 0.0)
    ones_q = ((clane >= ATTN_HEAD_DIM + 3) & (clane < ATTN_HEAD_DIM + 6)).astype(F32)
    ones_k = ((clane >= ATTN_HEAD_DIM) & (clane < ATTN_HEAD_DIM + 3)).astype(F32)
    ones = jnp.zeros((SUBLANES, ATTN_HEADS * AUG), F32).at[0].set(ones_q).at[1].set(ones_k)
    return pq.astype(BF16), pk.astype(BF16), ones


def _flash_kernel(qt_ref, k_ref, vt_ref, o_ref, m_ref, acc_ref, sa_ref, sb_ref):
    i = pl.program_id(2)
    t = qt_ref.shape[3]
    m_ref[...] = jnp.full_like(m_ref, -jnp.inf)
    acc_ref[...] = jnp.zeros_like(acc_ref)

    def scores(dst_ref, j):
        for h in range(2):
            k = k_ref[0, h, pl.ds(pl.multiple_of(j * t, t), t), :]
            dst_ref[h] = jnp.dot(k, qt_ref[0, h], preferred_element_type=F32)

    def softmax_pv(src_ref, j, masked):
        for h in range(2):
            s = src_ref[h]
            if masked:
                row = lax.broadcasted_iota(jnp.int32, (t, t), 0)
                col = lax.broadcasted_iota(jnp.int32, (t, t), 1)
                s = jnp.where(row <= col, s, -jnp.inf)
            m_prev = m_ref[h]
            m_next = jnp.maximum(m_prev, jnp.max(s, axis=0, keepdims=True))
            alpha = jnp.exp(m_prev - m_next)
            p = jnp.exp(s - m_next).astype(BF16)
            acc_ref[h] = alpha * acc_ref[h] + jnp.dot(vt_ref[0, h, j], p,
                                                      preferred_element_type=F32)
            m_ref[h] = m_next

    last = jnp.maximum(i - 1, 0)
    scores(sa_ref, i)
    scores(sb_ref, 0)
    softmax_pv(sa_ref, i, True)

    def body(jj, carry):
        j0 = 2 * jj
        scores(sa_ref, j0 + 1)
        softmax_pv(sb_ref, j0, False)
        scores(sb_ref, jnp.minimum(j0 + 2, last))
        softmax_pv(sa_ref, j0 + 1, False)
        return carry

    lax.fori_loop(0, i // 2, body, 0)

    @pl.when(i % 2 == 1)
    def _():
        softmax_pv(sb_ref, i - 1, False)

    outs = []
    for h in range(2):
        acc = acc_ref[h]
        outs.append(acc[0:ATTN_HEAD_DIM, :] / acc[ATTN_HEAD_DIM:ATTN_HEAD_DIM + 1, :])
    o_ref[...] = jnp.concatenate(outs, axis=0).T.astype(o_ref.dtype)


def _flash(qt, ka, vt, B, S, t):
    n = S // t
    T = B * S
    hp = ATTN_HEADS // 2
    return pl.pallas_call(
        _flash_kernel,
        grid=(B, hp, n),
        in_specs=[
            pl.BlockSpec((1, 2, AUG, t), lambda b, h, i: (b, h, 0, i)),
            pl.BlockSpec((1, 2, S, AUG), lambda b, h, i: (b, h, 0, 0)),
            pl.BlockSpec((1, 2, n, AUG, t), lambda b, h, i: (b, h, 0, 0, 0)),
        ],
        out_specs=pl.BlockSpec((t, LANES), lambda b, h, i: (b * n + i, h)),
        out_shape=jax.ShapeDtypeStruct((T, ATTN_WIDTH), BF16),
        scratch_shapes=[
            pltpu.VMEM((2, 1, t), F32),
            pltpu.VMEM((2, AUG, t), F32),
            pltpu.VMEM((2, t, t), F32),
            pltpu.VMEM((2, t, t), F32),
        ],
        compiler_params=_cparams(("parallel", "parallel", "arbitrary")),
        name="flash",
    )(qt, ka, vt)


PACK_STRIDE = 32


def _pack3(x, lane):
    hi, mid, lo = _split3(x)
    packed = jnp.where(lane < PACK_STRIDE, hi,
                       jnp.where(lane < 2 * PACK_STRIDE, pltpu.roll(mid, PACK_STRIDE, 1),
                                 pltpu.roll(lo, 2 * PACK_STRIDE, 1)))
    return packed.astype(BF16)


def _ssd_kernel(z_ref, xbc_ref, aux_ref, auxt_ref, cw_ref, cb_ref, dtb_ref, alog_ref,
                dtbt_ref, alogt_ref, dexp_ref, nw_ref, xp_ref,
                y_ref, xext_ref, state_ref):
    ts = z_ref.shape[0]
    L = SSD_CHUNK
    halo = SUBLANES

    @pl.when(pl.program_id(1) == 0)
    def _():
        state_ref[...] = jnp.zeros_like(state_ref)
        xext_ref[0:halo, :] = jnp.zeros((halo, SSD_CONV_CH), F32)

    xext_ref[halo:halo + ts, :] = xbc_ref[...]
    conv = cb_ref[...] + cw_ref[SSD_CONV - 1:SSD_CONV, :] * xext_ref[halo:halo + ts, :]
    for k in range(SSD_CONV - 1):
        off = halo - (SSD_CONV - 1) + k
        conv = conv + cw_ref[k:k + 1, :] * xext_ref[off:off + ts, :]
    xext_ref[0:halo, :] = xext_ref[ts:ts + halo, :]
    xc = jax.nn.silu(conv)

    dt = jax.nn.softplus(aux_ref[...] + dtb_ref[...])
    a = dt * (-jnp.exp(alog_ref[...]))
    dtt = jax.nn.softplus(auxt_ref[...] + dtbt_ref[...])
    at = dtt * (-jnp.exp(alogt_ref[...]))

    row = lax.broadcasted_iota(jnp.int32, (L, L), 0)
    col = lax.broadcasted_iota(jnp.int32, (L, L), 1)
    causal = col <= row
    tril = causal.astype(BF16)
    triu = (row <= col).astype(BF16)
    lane = lax.broadcasted_iota(jnp.int32, (L, LANES), 1)
    first_half = lane < SSD_HEAD_DIM
    xp = xp_ref[...]
    gw = SSD_WIDTH // SSD_GROUPS
    hpg = SSD_HEADS // SSD_GROUPS

    for c in range(ts // L):
        r = slice(c * L, (c + 1) * L)
        a_c = a[r]
        ah, am, al = _split3(a_c)
        a3 = jnp.concatenate([ah, am, al], axis=1).astype(BF16)
        cs3 = jnp.dot(tril, a3, preferred_element_type=F32)
        a_cs = cs3[:, :LANES] + cs3[:, LANES:2 * LANES] + cs3[:, 2 * LANES:]
        th, tm_, tl = _split3(at[:, r])
        t3 = jnp.concatenate([th, tm_, tl], axis=0).astype(BF16)
        cst = jnp.dot(t3, triu, preferred_element_type=F32)
        a_cst = cst[0:AUXT_ROWS] + cst[AUXT_ROWS:2 * AUXT_ROWS] + cst[2 * AUXT_ROWS:]

        e_small = jnp.exp(a_cs)
        w_small = jnp.exp(a_cs[L - 1:L, :] - a_cs) * dt[r]
        dt_exp = jnp.dot(_pack3(dt[r], lane), xp, preferred_element_type=F32)
        w_exp = jnp.dot(_pack3(w_small, lane), xp, preferred_element_type=F32)
        e_exp = jnp.dot(_pack3(e_small, lane), xp, preferred_element_type=F32)

        x_c = xc[r, 0:SSD_WIDTH]
        xd = (x_c * dt_exp).astype(BF16)
        xw = (x_c * w_exp).astype(BF16)
        y_parts = []
        for g in range(SSD_GROUPS):
            b_g = xc[r, SSD_WIDTH + g * SSD_STATE:SSD_WIDTH + (g + 1) * SSD_STATE]
            c_g = xc[r, SSD_WIDTH + (SSD_GROUPS + g) * SSD_STATE:
                     SSD_WIDTH + (SSD_GROUPS + g + 1) * SSD_STATE]
            c_gb = c_g.astype(BF16)
            cb = lax.dot_general(c_gb, b_g.astype(BF16), (((1,), (1,)), ((), ())),
                                 preferred_element_type=F32)
            gcols = slice(g * gw, (g + 1) * gw)
            st = state_ref[:, gcols]
            y_off = jnp.dot(c_gb, st.astype(BF16), preferred_element_type=F32) * e_exp[:, gcols]
            state_ref[:, gcols] = e_exp[L - 1:L, gcols] * st + jnp.dot(
                b_g.T.astype(BF16), xw[:, gcols], preferred_element_type=F32)
            for pr in range(hpg // 2):
                ms = []
                for e in range(2):
                    h = g * hpg + pr * 2 + e
                    seg = a_cs[:, AUX_DT0 + h:AUX_DT0 + h + 1] - a_cst[AUX_DT0 + h:AUX_DT0 + h + 1, :]
                    lm = jnp.exp(jnp.where(causal, seg, -jnp.inf))
                    ms.append((cb * lm).astype(BF16))
                lhs = jnp.concatenate(ms, axis=1)
                p0 = g * gw + pr * LANES
                xpair = xd[:, p0:p0 + LANES]
                zero = jnp.zeros_like(xpair)
                rhs = jnp.concatenate([jnp.where(first_half, xpair, zero),
                                       jnp.where(first_half, zero, xpair)], axis=0)
                y_parts.append(jnp.dot(lhs, rhs, preferred_element_type=F32)
                               + y_off[:, pr * LANES:(pr + 1) * LANES])
        y = jnp.concatenate(y_parts, axis=1) + dexp_ref[...] * x_c
        y = y * jax.nn.silu(z_ref[r, :])
        outs = []
        for g in range(SSD_GROUPS):
            yg = y[:, g * gw:(g + 1) * gw]
            yg = yg * lax.rsqrt(jnp.mean(yg * yg, axis=-1, keepdims=True) + NORM_EPS)
            outs.append(yg)
        y_ref[r, :] = (jnp.concatenate(outs, axis=1) * nw_ref[...]).astype(y_ref.dtype)


def _ssd(proj, auxt, cw, cb, dtb128, alog128, dtbt, alogt, dexp, nw, xp, B, S, ts=256):
    ns = S // ts
    T = B * S
    return pl.pallas_call(
        _ssd_kernel,
        grid=(B, ns),
        in_specs=[
            pl.BlockSpec((ts, SSD_WIDTH), lambda b, i: (b * ns + i, COL_Z // SSD_WIDTH)),
            pl.BlockSpec((ts, SSD_CONV_CH), lambda b, i: (b * ns + i, COL_XBC // SSD_CONV_CH)),
            pl.BlockSpec((ts, LANES), lambda b, i: (b * ns + i, COL_AUX // LANES)),
            pl.BlockSpec((AUXT_ROWS, ts), lambda b, i: (0, b * ns + i)),
            pl.BlockSpec((SSD_CONV, SSD_CONV_CH), lambda b, i: (0, 0)),
            pl.BlockSpec((1, SSD_CONV_CH), lambda b, i: (0, 0)),
            pl.BlockSpec((1, LANES), lambda b, i: (0, 0)),
            pl.BlockSpec((1, LANES), lambda b, i: (0, 0)),
            pl.BlockSpec((AUXT_ROWS, 1), lambda b, i: (0, 0)),
            pl.BlockSpec((AUXT_ROWS, 1), lambda b, i: (0, 0)),
            pl.BlockSpec((1, SSD_WIDTH), lambda b, i: (0, 0)),
            pl.BlockSpec((1, SSD_WIDTH), lambda b, i: (0, 0)),
            pl.BlockSpec((LANES, SSD_WIDTH), lambda b, i: (0, 0)),
        ],
        out_specs=pl.BlockSpec((ts, SSD_WIDTH), lambda b, i: (b * ns + i, 0)),
        out_shape=jax.ShapeDtypeStruct((T, SSD_WIDTH), BF16),
        scratch_shapes=[
            pltpu.VMEM((ts + SUBLANES, SSD_CONV_CH), F32),
            pltpu.VMEM((SSD_STATE, SSD_WIDTH), F32),
        ],
        compiler_params=_cparams(("parallel", "arbitrary")),
        name="ssd",
    )(proj, proj, proj, auxt, cw, cb, dtb128, alog128, dtbt, alogt, dexp, nw, xp)


def _ssd_expand_matrix():
    rows = jnp.arange(LANES)
    slot = rows % PACK_STRIDE
    part = rows // PACK_STRIDE
    head = slot - AUX_DT0
    cols = jnp.arange(SSD_WIDTH) // SSD_HEAD_DIM
    valid = (part < 3) & (head >= 0) & (head < SSD_HEADS)
    return jnp.where(valid[:, None] & (head[:, None] == cols[None, :]), 1.0, 0.0).astype(BF16)


POOL_HALO = 16


def _merge_kernel(x_ref, gates_ref, pool_ref, prev_ref, attn_ref, ssd_ref, mixw_ref, pscale_ref,
                  ppool_ref, pattn_ref, pssd_ref, wout_ref, o_ref, ext_ref):
    tm = x_ref.shape[0]
    i = pl.program_id(1)
    gdim = POOL_WIDTH // len(POOL_WINDOWS)

    prev = prev_ref[...]
    ext_ref[0:POOL_HALO, :] = jnp.where(i == 0, jnp.zeros_like(prev), prev)
    ext_ref[POOL_HALO:POOL_HALO + tm, :] = pool_ref[...]
    pos = i * tm + lax.broadcasted_iota(jnp.int32, (tm, gdim), 0)
    ys = []
    for g, w in enumerate(POOL_WINDOWS):
        cols = slice(g * gdim, (g + 1) * gdim)
        v = ext_ref[POOL_HALO:POOL_HALO + tm, cols]
        acc = v
        for k in range(1, w):
            acc = acc + ext_ref[POOL_HALO - k:POOL_HALO - k + tm, cols]
        cnt = jnp.minimum(pos + 1, w).astype(F32)
        d = (acc / cnt - v).astype(BF16)
        ys.append(jnp.dot(d, mixw_ref[g], preferred_element_type=F32))
    ypre = (jnp.concatenate(ys, axis=1) * pscale_ref[...]).astype(BF16)
    y_pool = jnp.dot(ypre, ppool_ref[...], preferred_element_type=F32)
    y_attn = jnp.dot(attn_ref[...], pattn_ref[...], preferred_element_type=F32)
    y_ssd = jnp.dot(ssd_ref[...], pssd_ref[...], preferred_element_type=F32)
    merged = (jax.nn.sigmoid(gates_ref[:, 0:D_MODEL]) * y_pool
              + jax.nn.sigmoid(gates_ref[:, D_MODEL:2 * D_MODEL]) * y_attn
              + jax.nn.sigmoid(gates_ref[:, 2 * D_MODEL:3 * D_MODEL]) * y_ssd)
    o_ref[...] = x_ref[...] + jnp.dot(merged.astype(BF16), wout_ref[...],
                                      preferred_element_type=F32)


def _merge(x2d, proj, attn, yssd, mixw, pscale, ppool, pattn, pssd, wout, B, S, tm=256):
    ns = S // tm
    T = B * S
    hb = tm // POOL_HALO
    const2 = lambda b, i: (0, 0)
    return pl.pallas_call(
        _merge_kernel,
        grid=(B, ns),
        in_specs=[
            pl.BlockSpec((tm, D_MODEL), lambda b, i: (b * ns + i, 0)),
            pl.BlockSpec((tm, 3 * D_MODEL), lambda b, i: (b * ns + i, 0)),
            pl.BlockSpec((tm, POOL_WIDTH), lambda b, i: (b * ns + i, COL_POOL // POOL_WIDTH)),
            pl.BlockSpec((POOL_HALO, POOL_WIDTH),
                         lambda b, i: (jnp.maximum((b * ns + i) * hb - 1, 0), COL_POOL // POOL_WIDTH)),
            pl.BlockSpec((tm, ATTN_WIDTH), lambda b, i: (b * ns + i, 0)),
            pl.BlockSpec((tm, SSD_WIDTH), lambda b, i: (b * ns + i, 0)),
            pl.BlockSpec((len(POOL_WINDOWS), LANES, LANES), lambda b, i: (0, 0, 0)),
            pl.BlockSpec((1, POOL_WIDTH), const2),
            pl.BlockSpec((POOL_WIDTH, D_MODEL), const2),
            pl.BlockSpec((ATTN_WIDTH, D_MODEL), const2),
            pl.BlockSpec((SSD_WIDTH, D_MODEL), const2),
            pl.BlockSpec((D_MODEL, D_MODEL), const2),
        ],
        out_specs=pl.BlockSpec((tm, D_MODEL), lambda b, i: (b * ns + i, 0)),
        out_shape=jax.ShapeDtypeStruct((T, D_MODEL), F32),
        scratch_shapes=[pltpu.VMEM((tm + POOL_HALO, POOL_WIDTH), F32)],
        compiler_params=_cparams(("parallel", "parallel")),
        name="merge",
    )(x2d, proj, proj, proj, attn, yssd, mixw, pscale, ppool, pattn, pssd, wout)


def _ffn_kernel(x_ref, nw_ref, wg_ref, wv_ref, cwg_ref, cwv_ref, cbg_ref, cbv_ref, wd_ref, fnw_ref,
                o_ref, u_ref, acc_ref, eg_ref, ev_ref, carg_ref, carv_ref, *, final_norm):
    tm = x_ref.shape[0]
    i = pl.program_id(1)
    j = pl.program_id(2)
    nf = pl.num_programs(2)
    halo = SUBLANES

    @pl.when(j == 0)
    def _():
        x = x_ref[...]
        ms = jnp.mean(x * x, axis=-1, keepdims=True)
        u_ref[...] = (x * lax.rsqrt(ms + NORM_EPS) * nw_ref[...]).astype(BF16)
        acc_ref[...] = jnp.zeros_like(acc_ref)

    @pl.when(i == 0)
    def _():
        carg_ref[j] = jnp.zeros(carg_ref.shape[1:], F32)
        carv_ref[j] = jnp.zeros(carv_ref.shape[1:], F32)

    u = u_ref[...]

    def conv_half(w_ref, ext_ref, car_ref, cw_ref, cb_ref):
        h = jnp.dot(u, w_ref[...], preferred_element_type=F32)
        ext_ref[0:halo, :] = car_ref[j]
        ext_ref[halo:halo + tm, :] = h
        car_ref[j] = h[tm - halo:tm, :]
        out = cb_ref[...] + cw_ref[FFN_CONV - 1:FFN_CONV, :] * h
        for k in range(FFN_CONV - 1):
            off = halo - (FFN_CONV - 1) + k
            out = out + cw_ref[k:k + 1, :] * ext_ref[off:off + tm, :]
        return out

    g = conv_half(wg_ref, eg_ref, carg_ref, cwg_ref, cbg_ref)
    v = conv_half(wv_ref, ev_ref, carv_ref, cwv_ref, cbv_ref)
    act = (jax.nn.silu(g) * v).astype(BF16)
    acc_ref[...] += jnp.dot(act, wd_ref[...], preferred_element_type=F32)

    @pl.when(j == nf - 1)
    def _():
        y = x_ref[...] + acc_ref[...]
        if final_norm:
            ms = jnp.mean(y * y, axis=-1, keepdims=True)
            y = y * lax.rsqrt(ms + NORM_EPS) * fnw_ref[...]
        o_ref[...] = y


def _ffn(x2d, nw, wup, cw, cb, wdown, fnw, B, S, final_norm, tm=512, tf=1408):
    ns = S // tm
    nf = FFN_DIM // tf
    T = B * S
    return pl.pallas_call(
        functools.partial(_ffn_kernel, final_norm=final_norm),
        grid=(B, ns, nf),
        in_specs=[
            pl.BlockSpec((tm, D_MODEL), lambda b, i, j: (b * ns + i, 0)),
            pl.BlockSpec((1, D_MODEL), lambda b, i, j: (0, 0)),
            pl.BlockSpec((D_MODEL, tf), lambda b, i, j: (0, j)),
            pl.BlockSpec((D_MODEL, tf), lambda b, i, j: (0, nf + j)),
            pl.BlockSpec((FFN_CONV, tf), lambda b, i, j: (0, j)),
            pl.BlockSpec((FFN_CONV, tf), lambda b, i, j: (0, nf + j)),
            pl.BlockSpec((1, tf), lambda b, i, j: (0, j)),
            pl.BlockSpec((1, tf), lambda b, i, j: (0, nf + j)),
            pl.BlockSpec((tf, D_MODEL), lambda b, i, j: (j, 0)),
            pl.BlockSpec((1, D_MODEL), lambda b, i, j: (0, 0)),
        ],
        out_specs=pl.BlockSpec((tm, D_MODEL), lambda b, i, j: (b * ns + i, 0)),
        out_shape=jax.ShapeDtypeStruct((T, D_MODEL), F32),
        scratch_shapes=[
            pltpu.VMEM((tm, D_MODEL), BF16),
            pltpu.VMEM((tm, D_MODEL), F32),
            pltpu.VMEM((tm + SUBLANES, tf), F32),
            pltpu.VMEM((tm + SUBLANES, tf), F32),
            pltpu.VMEM((nf, SUBLANES, tf), F32),
            pltpu.VMEM((nf, SUBLANES, tf), F32),
        ],
        compiler_params=_cparams(("parallel", "arbitrary", "arbitrary")),
        name="ffn",
    )(x2d, nw, wup, wup, cw, cw, cb, cb, wdown, fnw)


def _pad_lanes(v, offset, width=LANES):
    return jnp.zeros((1, width), F32).at[0, offset:offset + v.shape[0]].set(v.astype(F32))


def _permute_w_in(w):
    o = 0
    segs = {}
    for name, n in (("pool", 512), ("q", 512), ("k", 512), ("v", 512), ("f", ATTN_HEADS),
                    ("z", 1024), ("xbc", SSD_CONV_CH), ("dt", SSD_HEADS), ("gates", 3 * D_MODEL)):
        segs[name] = w[:, o:o + n]
        o += n
    small = jnp.concatenate([segs["f"], segs["dt"]], axis=1)
    aux = jnp.pad(small, ((0, 0), (0, PROJ_COLS - COL_AUX - small.shape[1])))
    w_perm = jnp.concatenate([segs["gates"], segs["z"], segs["pool"], segs["xbc"],
                              segs["q"], segs["k"], segs["v"], aux], axis=1).astype(BF16)
    wauxt = jnp.pad(small, ((0, 0), (0, AUXT_ROWS - small.shape[1]))).T.astype(BF16)
    return w_perm, wauxt


def kernel(x, norm_mix, w_in, pool_mix, pool_scale, f_bias, ssd_conv_w, ssd_conv_b, ssd_dt_bias,
           ssd_a_log, ssd_d, ssd_norm, p_pool, p_attn, p_ssd, w_out, norm_ffn, ffn_up, ffn_conv_w,
           ffn_conv_b, ffn_down, norm_final):
    B, S, D = x.shape
    depth = w_in.shape[0]
    T = B * S
    x2d = x.reshape(T, D)
    pq, pk, ones = _fox_constants()
    xp = _ssd_expand_matrix()
    fnw = norm_final.reshape(1, D)
    for l in range(depth):
        w_perm, wauxt = _permute_w_in(w_in[l])
        proj, auxt = _in_proj(x2d, norm_mix[l].reshape(1, D), w_perm, wauxt)
        qt, ka, vt = _fox_prep(proj, _pad_lanes(f_bias[l], AUX_F0), pq, pk, ones, B, S, ATTN_TILE)
        attn = _flash(qt, ka, vt, B, S, ATTN_TILE)
        dtb128 = _pad_lanes(ssd_dt_bias[l], AUX_DT0)
        alog128 = _pad_lanes(ssd_a_log[l], AUX_DT0)
        yssd = _ssd(proj, auxt, ssd_conv_w[l], ssd_conv_b[l].reshape(1, -1), dtb128, alog128,
                    _pad_lanes(ssd_dt_bias[l], AUX_DT0, AUXT_ROWS).T,
                    _pad_lanes(ssd_a_log[l], AUX_DT0, AUXT_ROWS).T,
                    jnp.repeat(ssd_d[l], SSD_HEAD_DIM).reshape(1, -1), ssd_norm[l].reshape(1, -1),
                    xp, B, S)
        x2d = _merge(x2d, proj, attn, yssd, pool_mix[l].astype(BF16), pool_scale[l].reshape(1, -1),
                     p_pool[l].astype(BF16), p_attn[l].astype(BF16), p_ssd[l].astype(BF16),
                     w_out[l].astype(BF16), B, S)
        x2d = _ffn(x2d, norm_ffn[l].reshape(1, D), ffn_up[l].astype(BF16), ffn_conv_w[l],
                   ffn_conv_b[l].reshape(1, -1), ffn_down[l].astype(BF16), fnw, B, S,
                   final_norm=(l == depth - 1))
    return x2d.reshape(B, S, D)
```

```python
import functools

import jax
import jax.numpy as jnp
from jax import lax
from jax.experimental import pallas as pl
from jax.experimental.pallas import tpu as pltpu

F32 = jnp.float32
BF16 = jnp.bfloat16

D_MODEL = 1024
NORM_EPS = 1e-6
POOL_WINDOWS = (2, 4, 8, 16)
POOL_WIDTH = 512
ATTN_HEADS = 8
ATTN_HEAD_DIM = 64
ATTN_WIDTH = 512
SSD_HEADS = 16
SSD_HEAD_DIM = 64
SSD_WIDTH = 1024
SSD_GROUPS = 2
SSD_STATE = 128
SSD_CONV = 4
SSD_CHUNK = 128
SSD_CONV_CH = 1536
FFN_DIM = 2816
FFN_CONV = 3

LANES = 128
SUBLANES = 8

COL_GATES = 0
COL_Z = 3072
COL_BC = 4096
COL_POOL = 4608
COL_X = 5120
COL_Q = 6144
COL_K = 6656
COL_V = 7168
PROJ_COLS = 7680
BC_WIDTH = 2 * SSD_GROUPS * SSD_STATE
PROJ_TILE = 1536
PROJ_CHUNK = 512
TILE_ZBC = COL_Z // PROJ_TILE
TILE_POOLX = COL_POOL // PROJ_TILE
assert COL_Z == TILE_ZBC * PROJ_TILE and COL_BC == COL_Z + SSD_WIDTH
assert COL_POOL == TILE_POOLX * PROJ_TILE and COL_X == COL_POOL + POOL_WIDTH
AUX_F0 = 0
AUX_DT0 = 8
AUXT_ROWS = 32

VMEM_LIMIT = 56 * 1024 * 1024


def _cparams(sem):
    return pltpu.CompilerParams(dimension_semantics=sem, vmem_limit_bytes=VMEM_LIMIT)


def _split3(x):
    hi = x.astype(BF16).astype(F32)
    r1 = x - hi
    mid = r1.astype(BF16).astype(F32)
    lo = (r1 - mid).astype(BF16).astype(F32)
    return hi, mid, lo


def _silu(x):
    h = 0.5 * x
    return h + h * jnp.tanh(h)


def _inproj_kernel(x_ref, nw_ref, w_ref, waux_ref, wauxt_ref, cw_ref, cb_ref,
                   out_ref, aux_ref, auxt_ref, u_ref, halo_ref, acc_ref, *, tiles_per_seq):
    i = pl.program_id(0)
    j = pl.program_id(1)
    tm = x_ref.shape[0]
    halo = SUBLANES

    @pl.when(j == 0)
    def _():
        x = x_ref[...]
        ms = jnp.mean(x * x, axis=-1, keepdims=True)
        u = (x * lax.rsqrt(ms + NORM_EPS) * nw_ref[...]).astype(BF16)
        u_ref[...] = u
        aux_ref[...] = jnp.dot(u, waux_ref[...], preferred_element_type=F32)
        auxt_ref[...] = lax.dot_general(wauxt_ref[...], u, (((1,), (1,)), ((), ())),
                                        preferred_element_type=F32)

    def conv_silu(acc, wcols):
        prev = halo_ref[:, wcols]
        prev = jnp.where(i % tiles_per_seq == 0, jnp.zeros_like(prev), prev)
        ext = jnp.concatenate([prev, acc], axis=0)
        halo_ref[:, wcols] = acc[tm - halo:tm, :]
        ext1 = pltpu.roll(ext, 1, 0)
        near = cw_ref[3:4, wcols] * ext + cw_ref[2:3, wcols] * ext1
        far = cw_ref[1:2, wcols] * ext + cw_ref[0:1, wcols] * ext1
        conv = (near + pltpu.roll(far, 2, 0))[halo:, :] + cb_ref[:, wcols]
        return _silu(conv)

    def tile(epilogues):
        def chunk_dot(c):
            acc_ref[c % 2] = jnp.dot(u_ref[...], w_ref[:, c * PROJ_CHUNK:(c + 1) * PROJ_CHUNK],
                                     preferred_element_type=F32)

        chunk_dot(0)
        for c, epilogue in enumerate(epilogues):
            if c + 1 < len(epilogues):
                chunk_dot(c + 1)
            out_ref[:, c * PROJ_CHUNK:(c + 1) * PROJ_CHUNK] = epilogue(
                acc_ref[c % 2]).astype(out_ref.dtype)

    def conv_at(off):
        return lambda acc: conv_silu(acc, slice(off, off + PROJ_CHUNK))

    plain = lambda acc: acc
    nchunk = out_ref.shape[1] // PROJ_CHUNK

    @pl.when((j != TILE_ZBC) & (j != TILE_POOLX))
    def _():
        tile([plain] * nchunk)

    @pl.when(j == TILE_ZBC)
    def _():
        tile([_silu, _silu, conv_at(SSD_WIDTH)])

    @pl.when(j == TILE_POOLX)
    def _():
        tile([plain, conv_at(0), conv_at(PROJ_CHUNK)])


def _in_proj(x2d, norm_w, w_perm, waux, wauxt, cw, cb, S, tm=1024, tn=PROJ_TILE):
    T = x2d.shape[0]
    return pl.pallas_call(
        functools.partial(_inproj_kernel, tiles_per_seq=S // tm),
        grid=(T // tm, PROJ_COLS // tn),
        in_specs=[
            pl.BlockSpec((tm, D_MODEL), lambda i, j: (i, 0)),
            pl.BlockSpec((1, D_MODEL), lambda i, j: (0, 0)),
            pl.BlockSpec((D_MODEL, tn), lambda i, j: (0, j)),
            pl.BlockSpec((D_MODEL, LANES), lambda i, j: (0, 0)),
            pl.BlockSpec((AUXT_ROWS, D_MODEL), lambda i, j: (0, 0)),
            pl.BlockSpec((SSD_CONV, SSD_CONV_CH), lambda i, j: (0, 0)),
            pl.BlockSpec((1, SSD_CONV_CH), lambda i, j: (0, 0)),
        ],
        out_specs=[
            pl.BlockSpec((tm, tn), lambda i, j: (i, j)),
            pl.BlockSpec((tm, LANES), lambda i, j: (i, 0)),
            pl.BlockSpec((AUXT_ROWS, tm), lambda i, j: (0, i)),
        ],
        out_shape=[
            jax.ShapeDtypeStruct((T, PROJ_COLS), BF16),
            jax.ShapeDtypeStruct((T, LANES), F32),
            jax.ShapeDtypeStruct((AUXT_ROWS, T), F32),
        ],
        scratch_shapes=[pltpu.VMEM((tm, D_MODEL), BF16),
                        pltpu.VMEM((SUBLANES, SSD_CONV_CH), F32),
                        pltpu.VMEM((2, tm, PROJ_CHUNK), F32)],
        compiler_params=_cparams(("arbitrary", "arbitrary")),
        name="in_proj",
    )(x2d, norm_w, w_perm, waux, wauxt, cw, cb)


AUG = 2 * ATTN_HEAD_DIM
ATTN_TILE = 512
VT_ROWS = ATTN_HEAD_DIM + 16


def _foxprep_kernel(q_ref, k_ref, v_ref, aux_ref, fb_ref, pq_ref, pk_ref, ones_ref,
                    qt_ref, ka_ref, vt_ref, carry_ref):
    ts = q_ref.shape[0]

    @pl.when(pl.program_id(1) == 0)
    def _():
        carry_ref[...] = jnp.zeros_like(carry_ref)

    logf = jax.nn.log_sigmoid(aux_ref[...] + fb_ref[...])
    row = lax.broadcasted_iota(jnp.int32, (ts, ts), 0)
    col = lax.broadcasted_iota(jnp.int32, (ts, ts), 1)
    tril = (col <= row).astype(BF16)
    hi, mid, lo = _split3(logf)
    x3 = jnp.concatenate([hi, mid, lo], axis=1).astype(BF16)
    cs3 = jnp.dot(tril, x3, preferred_element_type=F32)
    c = cs3[:, :LANES] + cs3[:, LANES:2 * LANES] + cs3[:, 2 * LANES:] + carry_ref[...]
    carry_ref[...] = c[ts - 1:ts, :]

    ch, cm, cl = _split3(c)
    c3 = jnp.concatenate([ch, cm, cl], axis=1).astype(BF16)
    augq = jnp.dot(c3, pq_ref[...], preferred_element_type=F32) + ones_ref[0:1, :]
    augk = jnp.dot(c3, pk_ref[...], preferred_element_type=F32) + ones_ref[1:2, :]
    lane = lax.broadcasted_iota(jnp.int32, (ts, LANES), 1)
    is_q = lane < ATTN_HEAD_DIM
    scale = ATTN_HEAD_DIM ** -0.5
    row_id = lax.broadcasted_iota(jnp.int32, (VT_ROWS - ATTN_HEAD_DIM, ts), 0)
    ones_rows = jnp.where(row_id == 0, 1.0, 0.0)
    for h in range(ATTN_HEADS):
        pair = slice((h // 2) * LANES, (h // 2 + 1) * LANES)
        blk = slice(h * LANES, (h + 1) * LANES)
        qp = q_ref[:, pair].astype(F32)
        kp = k_ref[:, pair].astype(F32)
        if h % 2 == 1:
            qp = pltpu.roll(qp, ATTN_HEAD_DIM, 1)
            kp = pltpu.roll(kp, ATTN_HEAD_DIM, 1)
        qa = jnp.where(is_q, qp * scale, augq[:, blk])
        qt_ref[0, h] = qa.T.astype(BF16)
        ka_ref[0, h] = jnp.where(is_q, kp, augk[:, blk]).astype(BF16)
        vt = v_ref[:, pair].astype(F32).T
        vh = vt[(h % 2) * ATTN_HEAD_DIM:(h % 2 + 1) * ATTN_HEAD_DIM, :]
        vt_ref[0, h, 0] = jnp.concatenate([vh, ones_rows], axis=0).astype(BF16)


def _fox_prep(proj, aux, fb128, pq, pk, ones, B, S, ts):
    ns = S // ts
    return pl.pallas_call(
        _foxprep_kernel,
        grid=(B, ns),
        in_specs=[
            pl.BlockSpec((ts, ATTN_WIDTH), lambda b, i: (b * ns + i, COL_Q // ATTN_WIDTH)),
            pl.BlockSpec((ts, ATTN_WIDTH), lambda b, i: (b * ns + i, COL_K // ATTN_WIDTH)),
            pl.BlockSpec((ts, ATTN_WIDTH), lambda b, i: (b * ns + i, COL_V // ATTN_WIDTH)),
            pl.BlockSpec((ts, LANES), lambda b, i: (b * ns + i, 0)),
            pl.BlockSpec((1, LANES), lambda b, i: (0, 0)),
            pl.BlockSpec((3 * LANES, ATTN_HEADS * AUG), lambda b, i: (0, 0)),
            pl.BlockSpec((3 * LANES, ATTN_HEADS * AUG), lambda b, i: (0, 0)),
            pl.BlockSpec((SUBLANES, ATTN_HEADS * AUG), lambda b, i: (0, 0)),
        ],
        out_specs=[
            pl.BlockSpec((1, ATTN_HEADS, AUG, ts), lambda b, i: (b, 0, 0, i)),
            pl.BlockSpec((1, ATTN_HEADS, ts, AUG), lambda b, i: (b, 0, i, 0)),
            pl.BlockSpec((1, ATTN_HEADS, 1, VT_ROWS, ts), lambda b, i: (b, 0, i, 0, 0)),
        ],
        out_shape=[
            jax.ShapeDtypeStruct((B, ATTN_HEADS, AUG, S), BF16),
            jax.ShapeDtypeStruct((B, ATTN_HEADS, S, AUG), BF16),
            jax.ShapeDtypeStruct((B, ATTN_HEADS, ns, VT_ROWS, ts), BF16),
        ],
        scratch_shapes=[pltpu.VMEM((1, LANES), F32)],
        compiler_params=_cparams(("parallel", "arbitrary")),
        name="fox_prep",
    )(proj, proj, proj, aux, fb128, pq, pk, ones)


def _fox_constants():
    rows = jnp.arange(3 * LANES)
    part = rows // LANES
    head = rows % LANES
    cols = jnp.arange(ATTN_HEADS * AUG)
    chead = cols // AUG
    clane = cols % AUG
    valid = (head[:, None] == chead[None, :]) & (head[:, None] < ATTN_HEADS)
    pq = jnp.where(valid & (clane[None, :] == ATTN_HEAD_DIM + part[:, None]), 1.0, 0.0)
    pk = jnp.where(valid & (clane[None, :] == ATTN_HEAD_DIM + 3 + part[:, None]), -1.0, 0.0)
    ones_q = ((clane >= ATTN_HEAD_DIM + 3) & (clane < ATTN_HEAD_DIM + 6)).astype(F32)
    ones_k = ((clane >= ATTN_HEAD_DIM) & (clane < ATTN_HEAD_DIM + 3)).astype(F32)
    ones = jnp.zeros((SUBLANES, ATTN_HEADS * AUG), F32).at[0].set(ones_q).at[1].set(ones_k)
    return pq.astype(BF16), pk.astype(BF16), ones


def _flash_kernel(qt_ref, k_ref, vt_ref, o_ref, m_ref, acc_ref, sa_ref, sb_ref):
    i = pl.program_id(2)
    t = qt_ref.shape[3]
    m_ref[...] = jnp.full_like(m_ref, -jnp.inf)
    acc_ref[...] = jnp.zeros_like(acc_ref)

    def scores(dst_ref, j):
        for h in range(2):
            k = k_ref[0, h, pl.ds(pl.multiple_of(j * t, t), t), :]
            dst_ref[h] = jnp.dot(k, qt_ref[0, h], preferred_element_type=F32)

    def softmax_pv(src_ref, j, masked):
        for h in range(2):
            s = src_ref[h]
            if masked:
                row = lax.broadcasted_iota(jnp.int32, (t, t), 0)
                col = lax.broadcasted_iota(jnp.int32, (t, t), 1)
                s = jnp.where(row <= col, s, -jnp.inf)
            m_prev = m_ref[h]
            m_next = jnp.maximum(m_prev, jnp.max(s, axis=0, keepdims=True))
            alpha = jnp.exp(m_prev - m_next)
            p = jnp.exp(s - m_next).astype(BF16)
            acc_ref[h] = alpha * acc_ref[h] + jnp.dot(vt_ref[0, h, j], p,
                                                      preferred_element_type=F32)
            m_ref[h] = m_next

    last = jnp.maximum(i - 1, 0)
    scores(sa_ref, i)
    scores(sb_ref, 0)
    softmax_pv(sa_ref, i, True)

    def body(jj, carry):
        j0 = 2 * jj
        scores(sa_ref, j0 + 1)
        softmax_pv(sb_ref, j0, False)
        scores(sb_ref, jnp.minimum(j0 + 2, last))
        softmax_pv(sa_ref, j0 + 1, False)
        return carry

    lax.fori_loop(0, i // 2, body, 0)

    @pl.when(i % 2 == 1)
    def _():
        softmax_pv(sb_ref, i - 1, False)

    outs = []
    for h in range(2):
        acc = acc_ref[h]
        outs.append(acc[0:ATTN_HEAD_DIM, :] / acc[ATTN_HEAD_DIM:ATTN_HEAD_DIM + 1, :])
    o_ref[...] = jnp.concatenate(outs, axis=0).T.astype(o_ref.dtype)


def _flash(qt, ka, vt, B, S, t):
    n = S // t
    T = B * S
    hp = ATTN_HEADS // 2
    return pl.pallas_call(
        _flash_kernel,
        grid=(B, hp, n),
        in_specs=[
            pl.BlockSpec((1, 2, AUG, t), lambda b, h, i: (b, h, 0, i)),
            pl.BlockSpec((1, 2, S, AUG), lambda b, h, i: (b, h, 0, 0)),
            pl.BlockSpec((1, 2, n, VT_ROWS, t), lambda b, h, i: (b, h, 0, 0, 0)),
        ],
        out_specs=pl.BlockSpec((t, LANES), lambda b, h, i: (b * n + i, h)),
        out_shape=jax.ShapeDtypeStruct((T, ATTN_WIDTH), BF16),
        scratch_shapes=[
            pltpu.VMEM((2, 1, t), F32),
            pltpu.VMEM((2, VT_ROWS, t), F32),
            pltpu.VMEM((2, t, t), F32),
            pltpu.VMEM((2, t, t), F32),
        ],
        compiler_params=_cparams(("parallel", "parallel", "arbitrary")),
        name="flash",
    )(qt, ka, vt)


PACK_STRIDE = 32
LOG2E = 1.4426950408889634


def _pack3(x, lane):
    hi, mid, lo = _split3(x)
    packed = jnp.where(lane < PACK_STRIDE, hi,
                       jnp.where(lane < 2 * PACK_STRIDE, pltpu.roll(mid, PACK_STRIDE, 1),
                                 pltpu.roll(lo, 2 * PACK_STRIDE, 1)))
    return packed.astype(BF16)


def _ssd_kernel(zs_ref, x_ref, bc_ref, aux_ref, auxt_ref, dtb_ref, alog_ref,
                dtbt_ref, alogt_ref, dexp_ref, nw_ref, xp_ref,
                y_ref, state_ref):
    ts = zs_ref.shape[0]
    L = SSD_CHUNK

    @pl.when(pl.program_id(1) == 0)
    def _():
        state_ref[...] = jnp.zeros_like(state_ref)

    dt =jax.nn.softplus(aux_ref[...] + dtb_ref[...])
    a = dt * (-jnp.exp(alog_ref[...]))
    dtt = jax.nn.softplus(auxt_ref[...] + dtbt_ref[...])
    at = dtt * (-jnp.exp(alogt_ref[...]))

    row = lax.broadcasted_iota(jnp.int32, (L, L), 0)
    col = lax.broadcasted_iota(jnp.int32, (L, L), 1)
    causal = col <= row
    tril = causal.astype(BF16)
    triu = (row <= col).astype(BF16)
    lane = lax.broadcasted_iota(jnp.int32, (L, LANES), 1)
    first_half = lane < SSD_HEAD_DIM
    xp = xp_ref[...]
    gw = SSD_WIDTH // SSD_GROUPS
    hpg = SSD_HEADS // SSD_GROUPS

    for c in range(ts // L):
        r = slice(c * L, (c + 1) * L)
        a_c = a[r]
        ah, am, al = _split3(a_c)
        a3 = jnp.concatenate([ah, am, al], axis=1).astype(BF16)
        cs3 = jnp.dot(tril, a3, preferred_element_type=F32)
        a_cs = cs3[:, :LANES] + cs3[:, LANES:2 * LANES] + cs3[:, 2 * LANES:]
        th, tm_, tl = _split3(at[:, r])
        t3 = jnp.concatenate([th, tm_, tl], axis=0).astype(BF16)
        cst = jnp.dot(t3, triu, preferred_element_type=F32)
        a_cst = cst[0:AUXT_ROWS] + cst[AUXT_ROWS:2 * AUXT_ROWS] + cst[2 * AUXT_ROWS:]

        a_l2 = a_cs * LOG2E
        a_l2t = a_cst * LOG2E
        e_small = jnp.exp(a_cs)
        w_small = jnp.exp(a_cs[L - 1:L, :] - a_cs) * dt[r]
        dt_exp = jnp.dot(_pack3(dt[r], lane), xp, preferred_element_type=F32)
        w_exp = jnp.dot(_pack3(w_small, lane), xp, preferred_element_type=F32)
        e_exp = jnp.dot(_pack3(e_small, lane), xp, preferred_element_type=F32)

        x_c = x_ref[r, :].astype(F32)
        xd = (x_c * dt_exp).astype(BF16)
        xw = (x_c * w_exp).astype(BF16)
        y_parts = []
        for g in range(SSD_GROUPS):
            b_gb = bc_ref[r, g * SSD_STATE:(g + 1) * SSD_STATE]
            c_gb = bc_ref[r, (SSD_GROUPS + g) * SSD_STATE:(SSD_GROUPS + g + 1) * SSD_STATE]
            cb = lax.dot_general(c_gb, b_gb, (((1,), (1,)), ((), ())),
                                 preferred_element_type=F32)
            gcols = slice(g * gw, (g + 1) * gw)
            st = state_ref[:, gcols]
            y_off = jnp.dot(c_gb, st.astype(BF16), preferred_element_type=F32) * e_exp[:, gcols]
            state_ref[:, gcols] = e_exp[L - 1:L, gcols] * st + jnp.dot(
                b_gb.astype(F32).T.astype(BF16), xw[:, gcols], preferred_element_type=F32)
            for pr in range(hpg // 2):
                ms = []
                for e in range(2):
                    h = g * hpg + pr * 2 + e
                    seg = a_l2[:, AUX_DT0 + h:AUX_DT0 + h + 1] - a_l2t[AUX_DT0 + h:AUX_DT0 + h + 1, :]
                    lm = jnp.exp2(jnp.where(causal, seg, -jnp.inf))
                    ms.append((cb * lm).astype(BF16))
                lhs = jnp.concatenate(ms, axis=1)
                p0 = g * gw + pr * LANES
                xpair = xd[:, p0:p0 + LANES]
                zero = jnp.zeros_like(xpair)
                rhs = jnp.concatenate([jnp.where(first_half, xpair, zero),
                                       jnp.where(first_half, zero, xpair)], axis=0)
                y_parts.append(jnp.dot(lhs, rhs, preferred_element_type=F32)
                               + y_off[:, pr * LANES:(pr + 1) * LANES])
        y = jnp.concatenate(y_parts, axis=1) + dexp_ref[...] * x_c
        y = y * zs_ref[r, :].astype(F32)
        outs = []
        for g in range(SSD_GROUPS):
            yg = y[:, g * gw:(g + 1) * gw]
            yg = yg * lax.rsqrt(jnp.mean(yg * yg, axis=-1, keepdims=True) + NORM_EPS)
            outs.append(yg)
        y_ref[r, :] = (jnp.concatenate(outs, axis=1) * nw_ref[...]).astype(y_ref.dtype)


def _ssd(proj, aux, auxt, dtb128, alog128, dtbt, alogt, dexp, nw, xp, B, S, ts=256):
    ns = S // ts
    T = B * S
    return pl.pallas_call(
        _ssd_kernel,
        grid=(B, ns),
        in_specs=[
            pl.BlockSpec((ts, SSD_WIDTH), lambda b, i: (b * ns + i, COL_Z // SSD_WIDTH)),
            pl.BlockSpec((ts, SSD_WIDTH), lambda b, i: (b * ns + i, COL_X // SSD_WIDTH)),
            pl.BlockSpec((ts, BC_WIDTH), lambda b, i: (b * ns + i, COL_BC // BC_WIDTH)),
            pl.BlockSpec((ts, LANES), lambda b, i: (b * ns + i, 0)),
            pl.BlockSpec((AUXT_ROWS, ts), lambda b, i: (0, b * ns + i)),
            pl.BlockSpec((1, LANES), lambda b, i: (0, 0)),
            pl.BlockSpec((1, LANES), lambda b, i: (0, 0)),
            pl.BlockSpec((AUXT_ROWS, 1), lambda b, i: (0, 0)),
            pl.BlockSpec((AUXT_ROWS, 1), lambda b, i: (0, 0)),
            pl.BlockSpec((1, SSD_WIDTH), lambda b, i: (0, 0)),
            pl.BlockSpec((1, SSD_WIDTH), lambda b, i: (0, 0)),
            pl.BlockSpec((LANES, SSD_WIDTH), lambda b, i: (0, 0)),
        ],
        out_specs=pl.BlockSpec((ts, SSD_WIDTH), lambda b, i: (b * ns + i, 0)),
        out_shape=jax.ShapeDtypeStruct((T, SSD_WIDTH), BF16),
        scratch_shapes=[pltpu.VMEM((SSD_STATE, SSD_WIDTH), F32)],
        compiler_params=_cparams(("parallel", "arbitrary")),
        name="ssd",
    )(proj, proj, proj, aux, auxt, dtb128, alog128, dtbt, alogt, dexp, nw, xp)


def _ssd_expand_matrix():
    rows = jnp.arange(LANES)
    slot = rows % PACK_STRIDE
    part = rows // PACK_STRIDE
    head = slot - AUX_DT0
    cols = jnp.arange(SSD_WIDTH) // SSD_HEAD_DIM
    valid = (part < 3) & (head >= 0) & (head < SSD_HEADS)
    return jnp.where(valid[:, None] & (head[:, None] == cols[None, :]), 1.0, 0.0).astype(BF16)


POOL_HALO = 16


def _merge_kernel(x_ref, gates_ref, pool_ref, prev_ref, attn_ref, ssd_ref, mixw_ref, pscale_ref,
                  ppool_ref, pattn_ref, pssd_ref, wout_ref, o_ref):
    tm = x_ref.shape[0]
    i = pl.program_id(1)
    gdim = POOL_WIDTH // len(POOL_WINDOWS)

    prev = prev_ref[...].astype(F32)
    prev = jnp.where(i == 0, jnp.zeros_like(prev), prev)
    pos = i * tm + lax.broadcasted_iota(jnp.int32, (tm, gdim), 0)
    ys = []
    for g, w in enumerate(POOL_WINDOWS):
        cols = slice(g * gdim, (g + 1) * gdim)
        v = pool_ref[:, cols].astype(F32)
        acc = jnp.concatenate([prev[:, cols], v], axis=0)
        span = 1
        while span < w:
            acc = acc + pltpu.roll(acc, span, 0)
            span *= 2
        cnt = jnp.minimum(pos + 1, w).astype(F32)
        d = (acc[POOL_HALO:, :] / cnt - v).astype(BF16)
        ys.append(jnp.dot(d, mixw_ref[g], preferred_element_type=F32))
    ypre = (jnp.concatenate(ys, axis=1) * pscale_ref[...]).astype(BF16)
    y_pool = jnp.dot(ypre, ppool_ref[...], preferred_element_type=F32)
    y_attn = jnp.dot(attn_ref[...], pattn_ref[...], preferred_element_type=F32)
    y_ssd = jnp.dot(ssd_ref[...], pssd_ref[...], preferred_element_type=F32)
    def gate(k):
        return jax.nn.sigmoid(gates_ref[:, k * D_MODEL:(k + 1) * D_MODEL].astype(F32))

    merged = gate(0) * y_pool + gate(1) * y_attn + gate(2) * y_ssd
    o_ref[...] = x_ref[...] + jnp.dot(merged.astype(BF16), wout_ref[...],
                                      preferred_element_type=F32)


def _merge(x2d, proj, attn, yssd, mixw, pscale, ppool, pattn, pssd, wout, B, S, tm=512):
    ns = S // tm
    T = B * S
    hb = tm // POOL_HALO
    const2 = lambda b, i: (0, 0)
    return pl.pallas_call(
        _merge_kernel,
        grid=(B, ns),
        in_specs=[
            pl.BlockSpec((tm, D_MODEL), lambda b, i: (b * ns + i, 0)),
            pl.BlockSpec((tm, 3 * D_MODEL), lambda b, i: (b * ns + i, 0)),
            pl.BlockSpec((tm, POOL_WIDTH), lambda b, i: (b * ns + i, COL_POOL // POOL_WIDTH)),
            pl.BlockSpec((POOL_HALO, POOL_WIDTH),
                         lambda b, i: (jnp.maximum((b * ns + i) * hb - 1, 0), COL_POOL // POOL_WIDTH)),
            pl.BlockSpec((tm, ATTN_WIDTH), lambda b, i: (b * ns + i, 0)),
            pl.BlockSpec((tm, SSD_WIDTH), lambda b, i: (b * ns + i, 0)),
            pl.BlockSpec((len(POOL_WINDOWS), LANES, LANES), lambda b, i: (0, 0, 0)),
            pl.BlockSpec((1, POOL_WIDTH), const2),
            pl.BlockSpec((POOL_WIDTH, D_MODEL), const2),
            pl.BlockSpec((ATTN_WIDTH, D_MODEL), const2),
            pl.BlockSpec((SSD_WIDTH, D_MODEL), const2),
            pl.BlockSpec((D_MODEL, D_MODEL), const2),
        ],
        out_specs=pl.BlockSpec((tm, D_MODEL), lambda b, i: (b * ns + i, 0)),
        out_shape=jax.ShapeDtypeStruct((T, D_MODEL), F32),
        compiler_params=_cparams(("parallel", "parallel")),
        name="merge",
    )(x2d, proj, proj, proj, attn, yssd, mixw, pscale, ppool, pattn, pssd, wout)


def _ffn_kernel(x_ref, nw_ref, wg_ref, wv_ref, cwg_ref, cwv_ref, cbg_ref, cbv_ref, wd_ref, fnw_ref,
                o_ref, u_ref, acc_ref, carg_ref, carv_ref, *, final_norm):
    tm = x_ref.shape[0]
    i = pl.program_id(1)
    j = pl.program_id(2)
    nf = pl.num_programs(2)
    halo = SUBLANES

    @pl.when(j == 0)
    def _():
        x = x_ref[...]
        ms = jnp.mean(x * x, axis=-1, keepdims=True)
        u_ref[...] = (x * lax.rsqrt(ms + NORM_EPS) * nw_ref[...]).astype(BF16)
        acc_ref[...] = jnp.zeros_like(acc_ref)

    @pl.when(i == 0)
    def _():
        carg_ref[j] = jnp.zeros(carg_ref.shape[1:], F32)
        carv_ref[j] = jnp.zeros(carv_ref.shape[1:], F32)

    u = u_ref[...]

    def conv_half(w_ref, car_ref, cw_ref, cb_ref):
        h = jnp.dot(u, w_ref[...], preferred_element_type=F32)
        ext = jnp.concatenate([car_ref[j], h], axis=0)
        car_ref[j] = h[tm - halo:tm, :]
        out = (cw_ref[2:3, :] * ext + cw_ref[1:2, :] * pltpu.roll(ext, 1, 0)
               + cw_ref[0:1, :] * pltpu.roll(ext, 2, 0))
        return out[halo:, :] + cb_ref[...]

    g = conv_half(wg_ref, carg_ref, cwg_ref, cbg_ref)
    v = conv_half(wv_ref, carv_ref, cwv_ref, cbv_ref)
    act = (_silu(g) * v).astype(BF16)
    acc_ref[...] += jnp.dot(act, wd_ref[...], preferred_element_type=F32)

    @pl.when(j == nf - 1)
    def _():
        y = x_ref[...] + acc_ref[...]
        if final_norm:
            ms = jnp.mean(y * y, axis=-1, keepdims=True)
            y = y * lax.rsqrt(ms + NORM_EPS) * fnw_ref[...]
        o_ref[...] = y


def _ffn(x2d, nw, wup, cw, cb, wdown, fnw, B, S, final_norm, tm=512, tf=1408):
    ns = S // tm
    nf = FFN_DIM // tf
    T = B * S
    return pl.pallas_call(
        functools.partial(_ffn_kernel, final_norm=final_norm),
        grid=(B, ns, nf),
        in_specs=[
            pl.BlockSpec((tm, D_MODEL), lambda b, i, j: (b * ns + i, 0)),
            pl.BlockSpec((1, D_MODEL), lambda b, i, j: (0, 0)),
            pl.BlockSpec((D_MODEL, tf), lambda b, i, j: (0, j)),
            pl.BlockSpec((D_MODEL, tf), lambda b, i, j: (0, nf + j)),
            pl.BlockSpec((FFN_CONV, tf), lambda b, i, j: (0, j)),
            pl.BlockSpec((FFN_CONV, tf), lambda b, i, j: (0, nf + j)),
            pl.BlockSpec((1, tf), lambda b, i, j: (0, j)),
            pl.BlockSpec((1, tf), lambda b, i, j: (0, nf + j)),
            pl.BlockSpec((tf, D_MODEL), lambda b, i, j: (j, 0)),
            pl.BlockSpec((1, D_MODEL), lambda b, i, j: (0, 0)),
        ],
        out_specs=pl.BlockSpec((tm, D_MODEL), lambda b, i, j: (b * ns + i, 0)),
        out_shape=jax.ShapeDtypeStruct((T, D_MODEL), F32),
        scratch_shapes=[
            pltpu.VMEM((tm, D_MODEL), BF16),
            pltpu.VMEM((tm, D_MODEL), F32),
            pltpu.VMEM((nf, SUBLANES, tf), F32),
            pltpu.VMEM((nf, SUBLANES, tf), F32),
        ],
        compiler_params=_cparams(("parallel", "arbitrary", "arbitrary")),
        name="ffn",
    )(x2d, nw, wup, wup, cw, cw, cb, cb, wdown, fnw)


def _pad_lanes(v, offset, width=LANES):
    return jnp.zeros((1, width), F32).at[0, offset:offset + v.shape[0]].set(v.astype(F32))


def _permute_w_in(w):
    o = 0
    segs = {}
    for name, n in (("pool", 512), ("q", 512), ("k", 512), ("v", 512), ("f", ATTN_HEADS),
                    ("z", 1024), ("xbc", SSD_CONV_CH), ("dt", SSD_HEADS), ("gates", 3 * D_MODEL)):
        segs[name] = w[:, o:o + n]
        o += n
    small = jnp.concatenate([segs["f"], segs["dt"]], axis=1)
    waux = jnp.pad(small, ((0, 0), (0, LANES - small.shape[1]))).astype(BF16)
    w_perm = jnp.concatenate([segs["gates"], segs["z"], segs["xbc"][:, SSD_WIDTH:],
                              segs["pool"], segs["xbc"][:, :SSD_WIDTH],
                              segs["q"], segs["k"], segs["v"]], axis=1).astype(BF16)
    wauxt = jnp.pad(small, ((0, 0), (0, AUXT_ROWS - small.shape[1]))).T.astype(BF16)
    return w_perm, waux, wauxt


def kernel(x, norm_mix, w_in, pool_mix, pool_scale, f_bias, ssd_conv_w, ssd_conv_b, ssd_dt_bias,
           ssd_a_log, ssd_d, ssd_norm, p_pool, p_attn, p_ssd, w_out, norm_ffn, ffn_up, ffn_conv_w,
           ffn_conv_b, ffn_down, norm_final):
    B, S, D = x.shape
    depth = w_in.shape[0]
    T = B * S
    x2d = x.reshape(T, D)
    pq, pk, ones = _fox_constants()
    xp = _ssd_expand_matrix()
    fnw = norm_final.reshape(1, D)
    for l in range(depth):
        w_perm, waux, wauxt = _permute_w_in(w_in[l])
        proj, aux, auxt = _in_proj(x2d, norm_mix[l].reshape(1, D), w_perm, waux, wauxt,
                                   ssd_conv_w[l], ssd_conv_b[l].reshape(1, -1), S)
        qt, ka, vt = _fox_prep(proj, aux, _pad_lanes(f_bias[l], AUX_F0), pq, pk, ones, B, S,
                               ATTN_TILE)
        attn = _flash(qt, ka, vt, B, S, ATTN_TILE)
        dtb128 = _pad_lanes(ssd_dt_bias[l], AUX_DT0)
        alog128 = _pad_lanes(ssd_a_log[l], AUX_DT0)
        yssd = _ssd(proj, aux, auxt, dtb128, alog128,
                    _pad_lanes(ssd_dt_bias[l], AUX_DT0, AUXT_ROWS).T,
                    _pad_lanes(ssd_a_log[l], AUX_DT0, AUXT_ROWS).T,
                    jnp.repeat(ssd_d[l], SSD_HEAD_DIM).reshape(1, -1), ssd_norm[l].reshape(1, -1),
                    xp, B, S)
        x2d = _merge(x2d, proj, attn, yssd, pool_mix[l].astype(BF16), pool_scale[l].reshape(1, -1),
                     p_pool[l].astype(BF16), p_attn[l].astype(BF16), p_ssd[l].astype(BF16),
                     w_out[l].astype(BF16), B, S)
        x2d = _ffn(x2d, norm_ffn[l].reshape(1, D), ffn_up[l].astype(BF16), ffn_conv_w[l],
                   ffn_conv_b[l].reshape(1, -1), ffn_down[l].astype(BF16), fnw, B, S,
                   final_norm=(l == depth - 1))
    return x2d.reshape(B, S, D)
```

```python
import functools

import jax
import jax.numpy as jnp
from jax import lax
from jax.experimental import pallas as pl
from jax.experimental.pallas import tpu as pltpu

F32 = jnp.float32
BF16 = jnp.bfloat16

D_MODEL = 1024
NORM_EPS = 1e-6
POOL_WINDOWS = (2, 4, 8, 16)
POOL_WIDTH = 512
ATTN_HEADS = 8
ATTN_HEAD_DIM = 64
ATTN_WIDTH = 512
SSD_HEADS = 16
SSD_HEAD_DIM = 64
SSD_WIDTH = 1024
SSD_GROUPS = 2
SSD_STATE = 128
SSD_CONV = 4
SSD_CHUNK = 128
SSD_CONV_CH = 1536
FFN_DIM = 2816
FFN_CONV = 3

LANES = 128
SUBLANES = 8

COL_GATES = 0
COL_Z = 3072
COL_BC = 4096
COL_POOL = 4608
COL_X = 5120
COL_Q = 6144
COL_K = 6656
COL_V = 7168
PROJ_COLS = 7680
BC_WIDTH = 2 * SSD_GROUPS * SSD_STATE
PROJ_TILE = 1536
PROJ_CHUNK = 512
TILE_ZBC = COL_Z // PROJ_TILE
TILE_POOLX = COL_POOL // PROJ_TILE
assert COL_Z == TILE_ZBC * PROJ_TILE and COL_BC == COL_Z + SSD_WIDTH
assert COL_POOL == TILE_POOLX * PROJ_TILE and COL_X == COL_POOL + POOL_WIDTH
AUX_F0 = 0
AUX_DT0 = 8
AUXT_ROWS = 32

VMEM_LIMIT = 56 * 1024 * 1024


def _cparams(sem):
    return pltpu.CompilerParams(dimension_semantics=sem, vmem_limit_bytes=VMEM_LIMIT)


def _split3(x):
    hi = x.astype(BF16).astype(F32)
    r1 = x - hi
    mid = r1.astype(BF16).astype(F32)
    lo = (r1 - mid).astype(BF16).astype(F32)
    return hi, mid, lo


def _silu(x):
    h = 0.5 * x
    return h + h * jnp.tanh(h)


def _inproj_kernel(x_ref, nw_ref, w_ref, waux_ref, wauxt_ref, cw_ref, cb_ref,
                   out_ref, aux_ref, auxt_ref, u_ref, halo_ref, acc_ref, *, tiles_per_seq):
    i = pl.program_id(0)
    j = pl.program_id(1)
    tm = x_ref.shape[0]
    halo = SUBLANES

    @pl.when(j == 0)
    def _():
        x = x_ref[...]
        ms = jnp.mean(x * x, axis=-1, keepdims=True)
        u = (x * lax.rsqrt(ms + NORM_EPS) * nw_ref[...]).astype(BF16)
        u_ref[...] = u
        aux_ref[...] = jnp.dot(u, waux_ref[...], preferred_element_type=F32)
        auxt_ref[...] = lax.dot_general(wauxt_ref[...], u, (((1,), (1,)), ((), ())),
                                        preferred_element_type=F32)

    def conv_silu(acc, wcols):
        prev = halo_ref[:, wcols]
        prev = jnp.where(i % tiles_per_seq == 0, jnp.zeros_like(prev), prev)
        ext = jnp.concatenate([prev, acc], axis=0)
        halo_ref[:, wcols] = acc[tm - halo:tm, :]
        ext1 = pltpu.roll(ext, 1, 0)
        near = cw_ref[3:4, wcols] * ext + cw_ref[2:3, wcols] * ext1
        far = cw_ref[1:2, wcols] * ext + cw_ref[0:1, wcols] * ext1
        conv = (near + pltpu.roll(far, 2, 0))[halo:, :] + cb_ref[:, wcols]
        return _silu(conv)

    def tile(epilogues):
        def chunk_dot(c):
            acc_ref[c % 2] = jnp.dot(u_ref[...], w_ref[:, c * PROJ_CHUNK:(c + 1) * PROJ_CHUNK],
                                     preferred_element_type=F32)

        chunk_dot(0)
        for c, epilogue in enumerate(epilogues):
            if c + 1 < len(epilogues):
                chunk_dot(c + 1)
            out_ref[:, c * PROJ_CHUNK:(c + 1) * PROJ_CHUNK] = epilogue(
                acc_ref[c % 2]).astype(out_ref.dtype)

    def conv_at(off):
        return lambda acc: conv_silu(acc, slice(off, off + PROJ_CHUNK))

    plain = lambda acc: acc
    nchunk = out_ref.shape[1] // PROJ_CHUNK

    @pl.when((j != TILE_ZBC) & (j != TILE_POOLX))
    def _():
        tile([plain] * nchunk)

    @pl.when(j == TILE_ZBC)
    def _():
        tile([_silu, _silu, conv_at(SSD_WIDTH)])

    @pl.when(j == TILE_POOLX)
    def _():
        tile([plain, conv_at(0), conv_at(PROJ_CHUNK)])


def _in_proj(x2d, norm_w, w_perm, waux, wauxt, layer, cw, cb, S, tm=1024, tn=PROJ_TILE):
    T = x2d.shape[0]
    return pl.pallas_call(
        functools.partial(_inproj_kernel, tiles_per_seq=S // tm),
        grid=(T // tm, PROJ_COLS // tn),
        in_specs=[
            pl.BlockSpec((tm, D_MODEL), lambda i, j: (i, 0)),
            pl.BlockSpec((1, D_MODEL), lambda i, j: (0, 0)),
            pl.BlockSpec((None, D_MODEL, tn), lambda i, j: (layer, 0, j)),
            pl.BlockSpec((None, D_MODEL, LANES), lambda i, j: (layer, 0, 0)),
            pl.BlockSpec((None, AUXT_ROWS, D_MODEL), lambda i, j: (layer, 0, 0)),
            pl.BlockSpec((SSD_CONV, SSD_CONV_CH), lambda i, j: (0, 0)),
            pl.BlockSpec((1, SSD_CONV_CH), lambda i, j: (0, 0)),
        ],
        out_specs=[
            pl.BlockSpec((tm, tn), lambda i, j: (i, j)),
            pl.BlockSpec((tm, LANES), lambda i, j: (i, 0)),
            pl.BlockSpec((AUXT_ROWS, tm), lambda i, j: (0, i)),
        ],
        out_shape=[
            jax.ShapeDtypeStruct((T, PROJ_COLS), BF16),
            jax.ShapeDtypeStruct((T, LANES), F32),
            jax.ShapeDtypeStruct((AUXT_ROWS, T), F32),
        ],
        scratch_shapes=[pltpu.VMEM((tm, D_MODEL), BF16),
                        pltpu.VMEM((SUBLANES, SSD_CONV_CH), F32),
                        pltpu.VMEM((2, tm, PROJ_CHUNK), F32)],
        compiler_params=_cparams(("arbitrary", "arbitrary")),
        name="in_proj",
    )(x2d, norm_w, w_perm, waux, wauxt, cw, cb)


AUG = 2 * ATTN_HEAD_DIM
ATTN_TILE = 512
VT_ROWS = ATTN_HEAD_DIM + 16


def _foxprep_kernel(q_ref, k_ref, v_ref, aux_ref, fb_ref, pq_ref, pk_ref, ones_ref,
                    qt_ref, ka_ref, vt_ref, carry_ref):
    ts = q_ref.shape[0]

    @pl.when(pl.program_id(1) == 0)
    def _():
        carry_ref[...] = jnp.zeros_like(carry_ref)

    logf = jax.nn.log_sigmoid(aux_ref[...] + fb_ref[...])
    row = lax.broadcasted_iota(jnp.int32, (ts, ts), 0)
    col = lax.broadcasted_iota(jnp.int32, (ts, ts), 1)
    tril = (col <= row).astype(BF16)
    hi, mid, lo = _split3(logf)
    x3 = jnp.concatenate([hi, mid, lo], axis=1).astype(BF16)
    cs3 = jnp.dot(tril, x3, preferred_element_type=F32)
    c = cs3[:, :LANES] + cs3[:, LANES:2 * LANES] + cs3[:, 2 * LANES:] + carry_ref[...]
    carry_ref[...] = c[ts - 1:ts, :]

    ch, cm, cl = _split3(c)
    c3 = jnp.concatenate([ch, cm, cl], axis=1).astype(BF16)
    augq = jnp.dot(c3, pq_ref[...], preferred_element_type=F32) + ones_ref[0:1, :]
    augk = jnp.dot(c3, pk_ref[...], preferred_element_type=F32) + ones_ref[1:2, :]
    lane = lax.broadcasted_iota(jnp.int32, (ts, LANES), 1)
    is_q = lane < ATTN_HEAD_DIM
    scale = ATTN_HEAD_DIM ** -0.5
    row_id = lax.broadcasted_iota(jnp.int32, (VT_ROWS - ATTN_HEAD_DIM, ts), 0)
    ones_rows = jnp.where(row_id == 0, 1.0, 0.0)
    for h in range(ATTN_HEADS):
        pair = slice((h // 2) * LANES, (h // 2 + 1) * LANES)
        blk = slice(h * LANES, (h + 1) * LANES)
        qp = q_ref[:, pair].astype(F32)
        kp = k_ref[:, pair].astype(F32)
        if h % 2 == 1:
            qp = pltpu.roll(qp, ATTN_HEAD_DIM, 1)
            kp = pltpu.roll(kp, ATTN_HEAD_DIM, 1)
        qa = jnp.where(is_q, qp * scale, augq[:, blk])
        qt_ref[0, h] = qa.T.astype(BF16)
        ka_ref[0, h] = jnp.where(is_q, kp, augk[:, blk]).astype(BF16)
        vt = v_ref[:, pair].astype(F32).T
        vh = vt[(h % 2) * ATTN_HEAD_DIM:(h % 2 + 1) * ATTN_HEAD_DIM, :]
        vt_ref[0, h, 0] = jnp.concatenate([vh, ones_rows], axis=0).astype(BF16)


def _fox_prep(proj, aux, fb128, pq, pk, ones, B, S, ts):
    ns = S // ts
    return pl.pallas_call(
        _foxprep_kernel,
        grid=(B, ns),
        in_specs=[
            pl.BlockSpec((ts, ATTN_WIDTH), lambda b, i: (b * ns + i, COL_Q // ATTN_WIDTH)),
            pl.BlockSpec((ts, ATTN_WIDTH), lambda b, i: (b * ns + i, COL_K // ATTN_WIDTH)),
            pl.BlockSpec((ts, ATTN_WIDTH), lambda b, i: (b * ns + i, COL_V // ATTN_WIDTH)),
            pl.BlockSpec((ts, LANES), lambda b, i: (b * ns + i, 0)),
            pl.BlockSpec((1, LANES), lambda b, i: (0, 0)),
            pl.BlockSpec((3 * LANES, ATTN_HEADS * AUG), lambda b, i: (0, 0)),
            pl.BlockSpec((3 * LANES, ATTN_HEADS * AUG), lambda b, i: (0, 0)),
            pl.BlockSpec((SUBLANES, ATTN_HEADS * AUG), lambda b, i: (0, 0)),
        ],
        out_specs=[
            pl.BlockSpec((1, ATTN_HEADS, AUG, ts), lambda b, i: (b, 0, 0, i)),
            pl.BlockSpec((1, ATTN_HEADS, ts, AUG), lambda b, i: (b, 0, i, 0)),
            pl.BlockSpec((1, ATTN_HEADS, 1, VT_ROWS, ts), lambda b, i: (b, 0, i, 0, 0)),
        ],
        out_shape=[
            jax.ShapeDtypeStruct((B, ATTN_HEADS, AUG, S), BF16),
            jax.ShapeDtypeStruct((B, ATTN_HEADS, S, AUG), BF16),
            jax.ShapeDtypeStruct((B, ATTN_HEADS, ns, VT_ROWS, ts), BF16),
        ],
        scratch_shapes=[pltpu.VMEM((1, LANES), F32)],
        compiler_params=_cparams(("parallel", "arbitrary")),
        name="fox_prep",
    )(proj, proj, proj, aux, fb128, pq, pk, ones)


def _fox_constants():
    rows = jnp.arange(3 * LANES)
    part = rows // LANES
    head = rows % LANES
    cols = jnp.arange(ATTN_HEADS * AUG)
    chead = cols // AUG
    clane = cols % AUG
    valid = (head[:, None] == chead[None, :]) & (head[:, None] < ATTN_HEADS)
    pq = jnp.where(valid & (clane[None, :] == ATTN_HEAD_DIM + part[:, None]), 1.0, 0.0)
    pk = jnp.where(valid & (clane[None, :] == ATTN_HEAD_DIM + 3 + part[:, None]), -1.0, 0.0)
    ones_q = ((clane >= ATTN_HEAD_DIM + 3) & (clane < ATTN_HEAD_DIM + 6)).astype(F32)
    ones_k = ((clane >= ATTN_HEAD_DIM) & (clane < ATTN_HEAD_DIM + 3)).astype(F32)
    ones = jnp.zeros((SUBLANES, ATTN_HEADS * AUG), F32).at[0].set(ones_q).at[1].set(ones_k)
    return pq.astype(BF16), pk.astype(BF16), ones


def _flash_kernel(qt_ref, k_ref, vt_ref, o_ref, m_ref, acc_ref, sa_ref, sb_ref):
    i = pl.program_id(2)
    t = qt_ref.shape[3]
    m_ref[...] = jnp.full_like(m_ref, -jnp.inf)
    acc_ref[...] = jnp.zeros_like(acc_ref)

    def scores(dst_ref, j):
        for h in range(2):
            k = k_ref[0, h, pl.ds(pl.multiple_of(j * t, t), t), :]
            dst_ref[h] = jnp.dot(k, qt_ref[0, h], preferred_element_type=F32)

    def softmax_pv(src_ref, j, masked):
        for h in range(2):
            s = src_ref[h]
            if masked:
                row = lax.broadcasted_iota(jnp.int32, (t, t), 0)
                col = lax.broadcasted_iota(jnp.int32, (t, t), 1)
                s = jnp.where(row <= col, s, -jnp.inf)
            m_prev = m_ref[h]
            m_next = jnp.maximum(m_prev, jnp.max(s, axis=0, keepdims=True))
            alpha = jnp.exp(m_prev - m_next)
            p = jnp.exp(s - m_next).astype(BF16)
            acc_ref[h] = alpha * acc_ref[h] + jnp.dot(vt_ref[0, h, j], p,
                                                      preferred_element_type=F32)
            m_ref[h] = m_next

    last = jnp.maximum(i - 1, 0)
    scores(sa_ref, i)
    scores(sb_ref, 0)
    softmax_pv(sa_ref, i, True)

    def body(jj, carry):
        j0 = 2 * jj
        scores(sa_ref, j0 + 1)
        softmax_pv(sb_ref, j0, False)
        scores(sb_ref, jnp.minimum(j0 + 2, last))
        softmax_pv(sa_ref, j0 + 1, False)
        return carry

    lax.fori_loop(0, i // 2, body, 0)

    @pl.when(i % 2 == 1)
    def _():
        softmax_pv(sb_ref, i - 1, False)

    outs = []
    for h in range(2):
        acc = acc_ref[h]
        outs.append(acc[0:ATTN_HEAD_DIM, :] / acc[ATTN_HEAD_DIM:ATTN_HEAD_DIM + 1, :])
    o_ref[...] = jnp.concatenate(outs, axis=0).T.astype(o_ref.dtype)


def _flash(qt, ka, vt, B, S, t):
    n = S // t
    T = B * S
    hp = ATTN_HEADS // 2
    return pl.pallas_call(
        _flash_kernel,
        grid=(B, hp, n),
        in_specs=[
            pl.BlockSpec((1, 2, AUG, t), lambda b, h, i: (b, h, 0, i)),
            pl.BlockSpec((1, 2, S, AUG), lambda b, h, i: (b, h, 0, 0)),
            pl.BlockSpec((1, 2, n, VT_ROWS, t), lambda b, h, i: (b, h, 0, 0, 0)),
        ],
        out_specs=pl.BlockSpec((t, LANES), lambda b, h, i: (b * n + i, h)),
        out_shape=jax.ShapeDtypeStruct((T, ATTN_WIDTH), BF16),
        scratch_shapes=[
            pltpu.VMEM((2, 1, t), F32),
            pltpu.VMEM((2, VT_ROWS, t), F32),
            pltpu.VMEM((2, t, t), F32),
            pltpu.VMEM((2, t, t), F32),
        ],
        compiler_params=_cparams(("parallel", "parallel", "arbitrary")),
        name="flash",
    )(qt, ka, vt)


PACK_STRIDE = 32
LOG2E = 1.4426950408889634


def _pack3(x, lane):
    hi, mid, lo = _split3(x)
    packed = jnp.where(lane < PACK_STRIDE, hi,
                       jnp.where(lane < 2 * PACK_STRIDE, pltpu.roll(mid, PACK_STRIDE, 1),
                                 pltpu.roll(lo, 2 * PACK_STRIDE, 1)))
    return packed.astype(BF16)


def _ssd_kernel(zs_ref, x_ref, bc_ref, aux_ref, auxt_ref, dtb_ref, alog_ref,
                dtbt_ref, alogt_ref, dexp_ref, nw_ref, xp_ref,
                y_ref, state_ref):
    ts = zs_ref.shape[0]
    L = SSD_CHUNK

    @pl.when(pl.program_id(1) == 0)
    def _():
        state_ref[...] = jnp.zeros_like(state_ref)

    dt =jax.nn.softplus(aux_ref[...] + dtb_ref[...])
    a = dt * (-jnp.exp(alog_ref[...]))
    dtt = jax.nn.softplus(auxt_ref[...] + dtbt_ref[...])
    at = dtt * (-jnp.exp(alogt_ref[...]))

    row = lax.broadcasted_iota(jnp.int32, (L, L), 0)
    col = lax.broadcasted_iota(jnp.int32, (L, L), 1)
    causal = col <= row
    tril = causal.astype(BF16)
    triu = (row <= col).astype(BF16)
    lane = lax.broadcasted_iota(jnp.int32, (L, LANES), 1)
    first_half = lane < SSD_HEAD_DIM
    xp = xp_ref[...]
    gw = SSD_WIDTH // SSD_GROUPS
    hpg = SSD_HEADS // SSD_GROUPS

    for c in range(ts // L):
        r = slice(c * L, (c + 1) * L)
        a_c = a[r]
        ah, am, al = _split3(a_c)
        a3 = jnp.concatenate([ah, am, al], axis=1).astype(BF16)
        cs3 = jnp.dot(tril, a3, preferred_element_type=F32)
        a_cs = cs3[:, :LANES] + cs3[:, LANES:2 * LANES] + cs3[:, 2 * LANES:]
        th, tm_, tl = _split3(at[:, r])
        t3 = jnp.concatenate([th, tm_, tl], axis=0).astype(BF16)
        cst = jnp.dot(t3, triu, preferred_element_type=F32)
        a_cst = cst[0:AUXT_ROWS] + cst[AUXT_ROWS:2 * AUXT_ROWS] + cst[2 * AUXT_ROWS:]

        a_l2 = a_cs * LOG2E
        a_l2t = a_cst * LOG2E
        e_small = jnp.exp(a_cs)
        w_small = jnp.exp(a_cs[L - 1:L, :] - a_cs) * dt[r]
        dt_exp = jnp.dot(_pack3(dt[r], lane), xp, preferred_element_type=F32)
        w_exp = jnp.dot(_pack3(w_small, lane), xp, preferred_element_type=F32)
        e_exp = jnp.dot(_pack3(e_small, lane), xp, preferred_element_type=F32)

        x_c = x_ref[r, :].astype(F32)
        xd = (x_c * dt_exp).astype(BF16)
        xw = (x_c * w_exp).astype(BF16)
        y_parts = []
        for g in range(SSD_GROUPS):
            b_gb = bc_ref[r, g * SSD_STATE:(g + 1) * SSD_STATE]
            c_gb = bc_ref[r, (SSD_GROUPS + g) * SSD_STATE:(SSD_GROUPS + g + 1) * SSD_STATE]
            cb = lax.dot_general(c_gb, b_gb, (((1,), (1,)), ((), ())),
                                 preferred_element_type=F32)
            gcols = slice(g * gw, (g + 1) * gw)
            st = state_ref[:, gcols]
            y_off = jnp.dot(c_gb, st.astype(BF16), preferred_element_type=F32) * e_exp[:, gcols]
            state_ref[:, gcols] = e_exp[L - 1:L, gcols] * st + jnp.dot(
                b_gb.astype(F32).T.astype(BF16), xw[:, gcols], preferred_element_type=F32)
            for pr in range(hpg // 2):
                ms = []
                for e in range(2):
                    h = g * hpg + pr * 2 + e
                    seg = a_l2[:, AUX_DT0 + h:AUX_DT0 + h + 1] - a_l2t[AUX_DT0 + h:AUX_DT0 + h + 1, :]
                    lm = jnp.exp2(jnp.where(causal, seg, -jnp.inf))
                    ms.append((cb * lm).astype(BF16))
                lhs = jnp.concatenate(ms, axis=1)
                p0 = g * gw + pr * LANES
                xpair = xd[:, p0:p0 + LANES]
                zero = jnp.zeros_like(xpair)
                rhs = jnp.concatenate([jnp.where(first_half, xpair, zero),
                                       jnp.where(first_half, zero, xpair)], axis=0)
                y_parts.append(jnp.dot(lhs, rhs, preferred_element_type=F32)
                               + y_off[:, pr * LANES:(pr + 1) * LANES])
        y = jnp.concatenate(y_parts, axis=1) + dexp_ref[...] * x_c
        y = y * zs_ref[r, :].astype(F32)
        outs = []
        for g in range(SSD_GROUPS):
            yg = y[:, g * gw:(g + 1) * gw]
            yg = yg * lax.rsqrt(jnp.mean(yg * yg, axis=-1, keepdims=True) + NORM_EPS)
            outs.append(yg)
        y_ref[r, :] = (jnp.concatenate(outs, axis=1) * nw_ref[...]).astype(y_ref.dtype)


def _ssd(proj, aux, auxt, dtb128, alog128, dtbt, alogt, dexp, nw, xp, B, S, ts=512):
    ns = S // ts
    T = B * S
    return pl.pallas_call(
        _ssd_kernel,
        grid=(B, ns),
        in_specs=[
            pl.BlockSpec((ts, SSD_WIDTH), lambda b, i: (b * ns + i, COL_Z // SSD_WIDTH)),
            pl.BlockSpec((ts, SSD_WIDTH), lambda b, i: (b * ns + i, COL_X // SSD_WIDTH)),
            pl.BlockSpec((ts, BC_WIDTH), lambda b, i: (b * ns + i, COL_BC // BC_WIDTH)),
            pl.BlockSpec((ts, LANES), lambda b, i: (b * ns + i, 0)),
            pl.BlockSpec((AUXT_ROWS, ts), lambda b, i: (0, b * ns + i)),
            pl.BlockSpec((1, LANES), lambda b, i: (0, 0)),
            pl.BlockSpec((1, LANES), lambda b, i: (0, 0)),
            pl.BlockSpec((AUXT_ROWS, 1), lambda b, i: (0, 0)),
            pl.BlockSpec((AUXT_ROWS, 1), lambda b, i: (0, 0)),
            pl.BlockSpec((1, SSD_WIDTH), lambda b, i: (0, 0)),
            pl.BlockSpec((1, SSD_WIDTH), lambda b, i: (0, 0)),
            pl.BlockSpec((LANES, SSD_WIDTH), lambda b, i: (0, 0)),
        ],
        out_specs=pl.BlockSpec((ts, SSD_WIDTH), lambda b, i: (b * ns + i, 0)),
        out_shape=jax.ShapeDtypeStruct((T, SSD_WIDTH), BF16),
        scratch_shapes=[pltpu.VMEM((SSD_STATE, SSD_WIDTH), F32)],
        compiler_params=_cparams(("parallel", "arbitrary")),
        name="ssd",
    )(proj, proj, proj, aux, auxt, dtb128, alog128, dtbt, alogt, dexp, nw, xp)


def _ssd_expand_matrix():
    rows = jnp.arange(LANES)
    slot = rows % PACK_STRIDE
    part = rows // PACK_STRIDE
    head = slot - AUX_DT0
    cols = jnp.arange(SSD_WIDTH) // SSD_HEAD_DIM
    valid = (part < 3) & (head >= 0) & (head < SSD_HEADS)
    return jnp.where(valid[:, None] & (head[:, None] == cols[None, :]), 1.0, 0.0).astype(BF16)


POOL_HALO = 16


def _merge_kernel(x_ref, gates_ref, pool_ref, prev_ref, attn_ref, ssd_ref, mixw_ref, pscale_ref,
                  ppool_ref, pattn_ref, pssd_ref, wout_ref, o_ref):
    tm = x_ref.shape[0]
    i = pl.program_id(1)
    gdim = POOL_WIDTH // len(POOL_WINDOWS)

    prev = prev_ref[...].astype(F32)
    prev = jnp.where(i == 0, jnp.zeros_like(prev), prev)
    head_pos = i * tm + lax.broadcasted_iota(jnp.int32, (POOL_HALO, gdim), 0)
    ys = []
    for g, w in enumerate(POOL_WINDOWS):
        assert w <= POOL_HALO and w & (w - 1) == 0
        cols = slice(g * gdim, (g + 1) * gdim)
        v = pool_ref[:, cols].astype(F32)
        acc = jnp.concatenate([prev[:, cols], v], axis=0)
        span = 1
        while span < w:
            acc = acc + pltpu.roll(acc, span, 0)
            span *= 2
        inv_cnt = jnp.concatenate(
            [1.0 / jnp.minimum(head_pos + 1, w).astype(F32),
             jnp.full((tm - POOL_HALO, gdim), 1.0 / w, F32)], axis=0)
        d = (acc[POOL_HALO:, :] * inv_cnt - v).astype(BF16)
        ys.append(jnp.dot(d, mixw_ref[g], preferred_element_type=F32))
    ypre = (jnp.concatenate(ys, axis=1) * pscale_ref[...]).astype(BF16)
    y_pool = jnp.dot(ypre, ppool_ref[...], preferred_element_type=F32)
    y_attn = jnp.dot(attn_ref[...], pattn_ref[...], preferred_element_type=F32)
    y_ssd = jnp.dot(ssd_ref[...], pssd_ref[...], preferred_element_type=F32)
    def gate(k):
        x = gates_ref[:, k * D_MODEL:(k + 1) * D_MODEL].astype(F32)
        return 0.5 * jnp.tanh(0.5 * x) + 0.5

    merged = gate(0) * y_pool + gate(1) * y_attn + gate(2) * y_ssd
    o_ref[...] = x_ref[...] + jnp.dot(merged.astype(BF16), wout_ref[...],
                                      preferred_element_type=F32)


def _merge(x2d, proj, attn, yssd, mixw, pscale, ppool, pattn, pssd, wout, layer, B, S, tm=512):
    ns = S // tm
    T = B * S
    hb = tm // POOL_HALO
    const2 = lambda b, i: (0, 0)
    stacked = lambda *shape: pl.BlockSpec((None,) + shape, lambda b, i: (layer,) + (0,) * len(shape))
    return pl.pallas_call(
        _merge_kernel,
        grid=(B, ns),
        in_specs=[
            pl.BlockSpec((tm, D_MODEL), lambda b, i: (b * ns + i, 0)),
            pl.BlockSpec((tm, 3 * D_MODEL), lambda b, i: (b * ns + i, 0)),
            pl.BlockSpec((tm, POOL_WIDTH), lambda b, i: (b * ns + i, COL_POOL // POOL_WIDTH)),
            pl.BlockSpec((POOL_HALO, POOL_WIDTH),
                         lambda b, i: (jnp.maximum((b * ns + i) * hb - 1, 0), COL_POOL // POOL_WIDTH)),
            pl.BlockSpec((tm, ATTN_WIDTH), lambda b, i: (b * ns + i, 0)),
            pl.BlockSpec((tm, SSD_WIDTH), lambda b, i: (b * ns + i, 0)),
            stacked(len(POOL_WINDOWS), LANES, LANES),
            pl.BlockSpec((1, POOL_WIDTH), const2),
            stacked(POOL_WIDTH, D_MODEL),
            stacked(ATTN_WIDTH, D_MODEL),
            stacked(SSD_WIDTH, D_MODEL),
            stacked(D_MODEL, D_MODEL),
        ],
        out_specs=pl.BlockSpec((tm, D_MODEL), lambda b, i: (b * ns + i, 0)),
        out_shape=jax.ShapeDtypeStruct((T, D_MODEL), F32),
        compiler_params=_cparams(("parallel", "parallel")),
        name="merge",
    )(x2d, proj, proj, proj, attn, yssd, mixw, pscale, ppool, pattn, pssd, wout)


FFN_CHUNK = 256


def _ffn_kernel(x_ref, nw_ref, wup_ref, cw_ref, cb_ref, wd_ref, fnw_ref,
                o_ref, u_ref, act_ref, stage_ref, car_ref, *, final_norm):
    tm = x_ref.shape[0]
    halo = SUBLANES
    nch = FFN_DIM // FFN_CHUNK

    x = x_ref[...]
    ms = jnp.mean(x * x, axis=-1, keepdims=True)
    u_ref[...] = (x * lax.rsqrt(ms + NORM_EPS) * nw_ref[...]).astype(BF16)

    @pl.when(pl.program_id(1) == 0)
    def _():
        car_ref[...] = jnp.zeros_like(car_ref)

    def cols_of(c, half):
        start = half * FFN_DIM + c * FFN_CHUNK
        return slice(start, start + FFN_CHUNK)

    def up(c):
        for half in range(2):
            stage_ref[2 * (c % 2) + half] = jnp.dot(u_ref[...], wup_ref[:, cols_of(c, half)],
                                                    preferred_element_type=F32)

    def conv(c, half):
        cols = cols_of(c, half)
        h = stage_ref[2 * (c % 2) + half]
        ext = jnp.concatenate([car_ref[:, cols], h], axis=0)
        car_ref[:, cols] = h[tm - halo:tm, :]
        out = (cw_ref[2:3, cols] * ext + cw_ref[1:2, cols] * pltpu.roll(ext, 1, 0)
               + cw_ref[0:1, cols] * pltpu.roll(ext, 2, 0))
        return out[halo:, :] + cb_ref[:, cols]

    up(0)
    for c in range(nch):
        if c + 1 < nch:
            up(c + 1)
        act_ref[:, c * FFN_CHUNK:(c + 1) * FFN_CHUNK] = (_silu(conv(c, 0)) * conv(c, 1)).astype(BF16)

    y = x_ref[...] + jnp.dot(act_ref[...], wd_ref[...], preferred_element_type=F32)
    if final_norm:
        ms = jnp.mean(y * y, axis=-1, keepdims=True)
        y = y * lax.rsqrt(ms + NORM_EPS) * fnw_ref[...]
    o_ref[...] = y


def _ffn(x2d, nw, wup, cw, cb, wdown, fnw, layer, B, S, final_norm, tm=512):
    ns = S // tm
    T = B * S
    whole = lambda shape: pl.BlockSpec(shape, lambda b, i: (0, 0))
    stacked = lambda *shape: pl.BlockSpec((None,) + shape, lambda b, i: (layer, 0, 0))
    return pl.pallas_call(
        functools.partial(_ffn_kernel, final_norm=final_norm),
        grid=(B, ns),
        in_specs=[
            pl.BlockSpec((tm, D_MODEL), lambda b, i: (b * ns + i, 0)),
            whole((1, D_MODEL)),
            stacked(D_MODEL, 2 * FFN_DIM),
            whole((FFN_CONV, 2 * FFN_DIM)),
            whole((1, 2 * FFN_DIM)),
            stacked(FFN_DIM, D_MODEL),
            whole((1, D_MODEL)),
        ],
        out_specs=pl.BlockSpec((tm, D_MODEL), lambda b, i: (b * ns + i, 0)),
        out_shape=jax.ShapeDtypeStruct((T, D_MODEL), F32),
        scratch_shapes=[
            pltpu.VMEM((tm, D_MODEL), BF16),
            pltpu.VMEM((tm, FFN_DIM), BF16),
            pltpu.VMEM((4, tm, FFN_CHUNK), F32),
            pltpu.VMEM((SUBLANES, 2 * FFN_DIM), F32),
        ],
        compiler_params=_cparams(("parallel", "arbitrary")),
        name="ffn",
    )(x2d, nw, wup, cw, cb, wdown, fnw)


def _pad_lanes(v, offset, width=LANES):
    return jnp.zeros((1, width), F32).at[0, offset:offset + v.shape[0]].set(v.astype(F32))


IN_POOL, IN_QKV, IN_F, IN_Z, IN_X, IN_BC, IN_DT, IN_GATES = 0, 512, 2048, 2056, 3080, 4104, 4616, 4632
IN_TOTAL = IN_GATES + 3 * D_MODEL


def _wprep_kernel(w_ref, wp_ref, waux_ref):
    seg = lambda start, n: w_ref[:, start:start + n]
    wp_ref[...] = jnp.concatenate(
        [seg(IN_GATES, 3 * D_MODEL), seg(IN_Z, SSD_WIDTH), seg(IN_BC, BC_WIDTH),
         seg(IN_POOL, POOL_WIDTH), seg(IN_X, SSD_WIDTH), seg(IN_QKV, 3 * ATTN_WIDTH)],
        axis=1).astype(BF16)
    lane = lax.broadcasted_iota(jnp.int32, (w_ref.shape[0], LANES), 1)
    dt_group = IN_DT - AUX_DT0
    waux_ref[...] = jnp.where(lane < AUX_DT0, seg(IN_F, LANES),
                              jnp.where(lane < AUX_DT0 + SSD_HEADS, seg(dt_group, LANES),
                                        0.0)).astype(BF16)


def _wprep(w_in, tr=256):
    L, K, N = w_in.shape
    assert N == IN_TOTAL and IN_F % LANES == 0 and (IN_DT - AUX_DT0) % LANES == 0
    return pl.pallas_call(
        _wprep_kernel,
        grid=(L, K // tr),
        in_specs=[pl.BlockSpec((None, tr, N), lambda l, i: (l, i, 0))],
        out_specs=[pl.BlockSpec((None, tr, PROJ_COLS), lambda l, i: (l, i, 0)),
                   pl.BlockSpec((None, tr, LANES), lambda l, i: (l, i, 0))],
        out_shape=[jax.ShapeDtypeStruct((L, K, PROJ_COLS), BF16),
                   jax.ShapeDtypeStruct((L, K, LANES), BF16)],
        compiler_params=_cparams(("parallel", "parallel")),
        name="wprep",
    )(w_in)


def kernel(x, norm_mix, w_in, pool_mix, pool_scale, f_bias, ssd_conv_w, ssd_conv_b, ssd_dt_bias,
           ssd_a_log, ssd_d, ssd_norm, p_pool, p_attn, p_ssd, w_out, norm_ffn, ffn_up, ffn_conv_w,
           ffn_conv_b, ffn_down, norm_final):
    B, S, D = x.shape
    depth = w_in.shape[0]
    T = B * S
    x2d = x.reshape(T, D)
    pq, pk, ones = _fox_constants()
    xp = _ssd_expand_matrix()
    fnw = norm_final.reshape(1, D)
    w_perm, waux = _wprep(w_in)
    wauxt = jnp.swapaxes(waux[:, :, :AUXT_ROWS], 1, 2)
    mixw, ppool, pattn, pssd, wout, wup, wdown = (
        w.astype(BF16) for w in (pool_mix, p_pool, p_attn, p_ssd, w_out, ffn_up, ffn_down))
    for l in range(depth):
        proj, aux, auxt = _in_proj(x2d, norm_mix[l].reshape(1, D), w_perm, waux, wauxt, l,
                                   ssd_conv_w[l], ssd_conv_b[l].reshape(1, -1), S)
        qt, ka, vt = _fox_prep(proj, aux, _pad_lanes(f_bias[l], AUX_F0), pq, pk, ones, B, S,
                               ATTN_TILE)
        attn = _flash(qt, ka, vt, B, S, ATTN_TILE)
        dtb128 = _pad_lanes(ssd_dt_bias[l], AUX_DT0)
        alog128 = _pad_lanes(ssd_a_log[l], AUX_DT0)
        yssd = _ssd(proj, aux, auxt, dtb128, alog128,
                    _pad_lanes(ssd_dt_bias[l], AUX_DT0, AUXT_ROWS).T,
                    _pad_lanes(ssd_a_log[l], AUX_DT0, AUXT_ROWS).T,
                    jnp.repeat(ssd_d[l], SSD_HEAD_DIM).reshape(1, -1), ssd_norm[l].reshape(1, -1),
                    xp, B, S)
        x2d = _merge(x2d, proj, attn, yssd, mixw, pool_scale[l].reshape(1, -1),
                     ppool, pattn, pssd, wout, l, B, S)
        x2d = _ffn(x2d, norm_ffn[l].reshape(1, D), wup, ffn_conv_w[l],
                   ffn_conv_b[l].reshape(1, -1), wdown, fnw, l, B, S,
                   final_norm=(l == depth - 1))
    return x2d.reshape(B, S, D)
```

```python
import functools

import jax
import jax.numpy as jnp
from jax import lax
from jax.experimental import pallas as pl
from jax.experimental.pallas import tpu as pltpu

F32 = jnp.float32
BF16 = jnp.bfloat16

D_MODEL = 1024
NORM_EPS = 1e-6
POOL_WINDOWS = (2, 4, 8, 16)
POOL_WIDTH = 512
ATTN_HEADS = 8
ATTN_HEAD_DIM = 64
ATTN_WIDTH = 512
SSD_HEADS = 16
SSD_HEAD_DIM = 64
SSD_WIDTH = 1024
SSD_GROUPS = 2
SSD_STATE = 128
SSD_CONV = 4
SSD_CHUNK = 128
SSD_CONV_CH = 1536
FFN_DIM = 2816
FFN_CONV = 3

LANES = 128
SUBLANES = 8

COL_GATES = 0
COL_Z = 3072
COL_BC = 4096
COL_POOL = 4608
COL_X = 5120
COL_Q = 6144
COL_K = 6656
COL_V = 7168
PROJ_COLS = 7680
BC_WIDTH = 2 * SSD_GROUPS * SSD_STATE
PROJ_TILE = 1536
PROJ_CHUNK = 256
TILE_ZBC = COL_Z // PROJ_TILE
TILE_POOLX = COL_POOL // PROJ_TILE
assert COL_Z == TILE_ZBC * PROJ_TILE and COL_BC == COL_Z + SSD_WIDTH
assert COL_POOL == TILE_POOLX * PROJ_TILE and COL_X == COL_POOL + POOL_WIDTH
AUX_F0 = 0
AUX_DT0 = 8
AUXT_ROWS = 32

VMEM_LIMIT = 56 * 1024 * 1024


def _cparams(sem):
    return pltpu.CompilerParams(dimension_semantics=sem, vmem_limit_bytes=VMEM_LIMIT)


def _split3(x):
    hi = x.astype(BF16).astype(F32)
    r1 = x - hi
    mid = r1.astype(BF16).astype(F32)
    lo = (r1 - mid).astype(BF16).astype(F32)
    return hi, mid, lo


def _pack3(x, lane, stride):
    hi, mid, lo = _split3(x)
    packed = jnp.where(lane < stride, hi,
                       jnp.where(lane < 2 * stride, pltpu.roll(mid, stride, 1),
                                 pltpu.roll(lo, 2 * stride, 1)))
    return packed.astype(BF16)


def _silu(x):
    h = 0.5 * x
    return h + h * jnp.tanh(h)


def _inproj_kernel(x_ref, nw_ref, w_ref, wauxt_ref, cw_ref, cb_ref,
                   out_ref, aux_ref, auxt_ref, u_ref, halo_ref, acc_ref, *, tiles_per_seq):
    i = pl.program_id(0)
    j = pl.program_id(1)
    tm = x_ref.shape[0]
    halo = SUBLANES

    @pl.when(j == 0)
    def _():
        x = x_ref[...]
        ms = jnp.mean(x * x, axis=-1, keepdims=True)
        u = (x * lax.rsqrt(ms + NORM_EPS) * nw_ref[...]).astype(BF16)
        u_ref[...] = u
        auxt = lax.dot_general(wauxt_ref[...], u, (((1,), (1,)), ((), ())),
                               preferred_element_type=F32)
        auxt_ref[...] = auxt
        aux_ref[...] = jnp.concatenate(
            [auxt, jnp.zeros((LANES - AUXT_ROWS, tm), F32)], axis=0).T

    def conv_silu(acc, wcols):
        prev = halo_ref[:, wcols]
        prev = jnp.where(i % tiles_per_seq == 0, jnp.zeros_like(prev), prev)
        ext = jnp.concatenate([prev, acc], axis=0)
        halo_ref[:, wcols] = acc[tm - halo:tm, :]
        ext1 = pltpu.roll(ext, 1, 0)
        near = cw_ref[3:4, wcols] * ext + cw_ref[2:3, wcols] * ext1
        far = cw_ref[1:2, wcols] * ext + cw_ref[0:1, wcols] * ext1
        conv = (near + pltpu.roll(far, 2, 0))[halo:, :] + cb_ref[:, wcols]
        return _silu(conv)

    def tile(epilogues):
        def chunk_dot(c):
            acc_ref[c % 2] = jnp.dot(u_ref[...], w_ref[:, c * PROJ_CHUNK:(c + 1) * PROJ_CHUNK],
                                     preferred_element_type=F32)

        chunk_dot(0)
        for c, epilogue in enumerate(epilogues):
            if c + 1 < len(epilogues):
                chunk_dot(c + 1)
            out_ref[:, c * PROJ_CHUNK:(c + 1) * PROJ_CHUNK] = epilogue(
                acc_ref[c % 2]).astype(out_ref.dtype)

    def conv_at(off):
        return lambda acc: conv_silu(acc, slice(off, off + PROJ_CHUNK))

    plain = lambda acc: acc
    nchunk = out_ref.shape[1] // PROJ_CHUNK

    @pl.when((j != TILE_ZBC) & (j != TILE_POOLX))
    def _():
        tile([plain] * nchunk)

    def convs(first, width):
        return [conv_at(first + k * PROJ_CHUNK) for k in range(width // PROJ_CHUNK)]

    @pl.when(j == TILE_ZBC)
    def _():
        tile([_silu] * (SSD_WIDTH // PROJ_CHUNK) + convs(SSD_WIDTH, BC_WIDTH))

    @pl.when(j == TILE_POOLX)
    def _():
        tile([plain] * (POOL_WIDTH // PROJ_CHUNK) + convs(0, SSD_WIDTH))


def _in_proj(x2d, norm_w, w_perm, wauxt, layer, cw, cb, S, tm=1024, tn=PROJ_TILE):
    T = x2d.shape[0]
    return pl.pallas_call(
        functools.partial(_inproj_kernel, tiles_per_seq=S // tm),
        grid=(T // tm, PROJ_COLS // tn),
        in_specs=[
            pl.BlockSpec((tm, D_MODEL), lambda i, j: (i, 0)),
            pl.BlockSpec((1, D_MODEL), lambda i, j: (0, 0)),
            pl.BlockSpec((None, D_MODEL, tn), lambda i, j: (layer, 0, j)),
            pl.BlockSpec((None, AUXT_ROWS, D_MODEL), lambda i, j: (layer, 0, 0)),
            pl.BlockSpec((SSD_CONV, SSD_CONV_CH), lambda i, j: (0, 0)),
            pl.BlockSpec((1, SSD_CONV_CH), lambda i, j: (0, 0)),
        ],
        out_specs=[
            pl.BlockSpec((tm, tn), lambda i, j: (i, j)),
            pl.BlockSpec((tm, LANES), lambda i, j: (i, 0)),
            pl.BlockSpec((AUXT_ROWS, tm), lambda i, j: (0, i)),
        ],
        out_shape=[
            jax.ShapeDtypeStruct((T, PROJ_COLS), BF16),
            jax.ShapeDtypeStruct((T, LANES), F32),
            jax.ShapeDtypeStruct((AUXT_ROWS, T), F32),
        ],
        scratch_shapes=[pltpu.VMEM((tm, D_MODEL), BF16),
                        pltpu.VMEM((SUBLANES, SSD_CONV_CH), F32),
                        pltpu.VMEM((2, tm, PROJ_CHUNK), F32)],
        compiler_params=_cparams(("arbitrary", "arbitrary")),
        name="in_proj",
    )(x2d, norm_w, w_perm, wauxt, cw, cb)


AUG = 2 * ATTN_HEAD_DIM
ATTN_TILE = 512
VT_ROWS = ATTN_HEAD_DIM + 16


def _foxprep_kernel(q_ref, k_ref, v_ref, auxt_ref, fbt_ref, triu_ref, place_ref,
                    qt_ref, ka_ref, vt_ref, carry_ref):
    ts = q_ref.shape[0]
    nh = ATTN_HEADS
    R = AUXT_ROWS

    @pl.when(pl.program_id(1) == 0)
    def _():
        carry_ref[...] = jnp.zeros_like(carry_ref)

    logf = jax.nn.log_sigmoid(auxt_ref[...] + fbt_ref[...])
    parts = jnp.concatenate(_split3(logf), axis=0).astype(BF16)
    cs = jnp.dot(parts, triu_ref[...], preferred_element_type=F32)
    ct = cs[0:R] + cs[R:2 * R] + cs[2 * R:3 * R] + carry_ref[...]
    carry_ref[...] = ct[:, ts - 1:ts]

    ch, cm, cl = _split3(ct)
    stacked = jnp.concatenate([ch[0:nh], cm[0:nh], cl[0:nh],
                               jnp.zeros((LANES - 3 * nh, ts), F32)], axis=0)
    aug = jnp.dot(stacked.T.astype(BF16), place_ref[...], preferred_element_type=F32)
    lane = lax.broadcasted_iota(jnp.int32, (ts, LANES), 1)
    is_q = lane < ATTN_HEAD_DIM
    is_pos = lane < ATTN_HEAD_DIM + 3
    ones_hi = jnp.where((lane >= ATTN_HEAD_DIM + 3) & (lane < ATTN_HEAD_DIM + 6), 1.0, 0.0)
    ones_lo = jnp.where((lane >= ATTN_HEAD_DIM) & is_pos, 1.0, 0.0)
    scale = ATTN_HEAD_DIM ** -0.5
    row_id = lax.broadcasted_iota(jnp.int32, (VT_ROWS - ATTN_HEAD_DIM, ts), 0)
    ones_rows = jnp.where(row_id == 0, 1.0, 0.0)
    for h in range(ATTN_HEADS):
        pair = slice((h // 2) * LANES, (h // 2 + 1) * LANES)
        blk = slice(h * LANES, (h + 1) * LANES)
        qp = q_ref[:, pair].astype(F32)
        kp = k_ref[:, pair].astype(F32)
        if h % 2 == 1:
            qp = pltpu.roll(qp, ATTN_HEAD_DIM, 1)
            kp = pltpu.roll(kp, ATTN_HEAD_DIM, 1)
        qa = jnp.where(is_q, qp * scale, jnp.where(is_pos, aug[:, blk], ones_hi))
        qt_ref[0, h] = qa.T.astype(BF16)
        ka_ref[0, h] = jnp.where(is_q, kp, jnp.where(is_pos, ones_lo, aug[:, blk])).astype(BF16)
        vt = v_ref[:, pair].astype(F32).T
        vh = vt[(h % 2) * ATTN_HEAD_DIM:(h % 2 + 1) * ATTN_HEAD_DIM, :]
        vt_ref[0, h, 0] = jnp.concatenate([vh, ones_rows], axis=0).astype(BF16)


def _fox_prep(proj, auxt, fbt, place, B, S, ts):
    ns = S // ts
    triu = (jnp.arange(ts)[:, None] <= jnp.arange(ts)[None, :]).astype(BF16)
    return pl.pallas_call(
        _foxprep_kernel,
        grid=(B, ns),
        in_specs=[
            pl.BlockSpec((ts, ATTN_WIDTH), lambda b, i: (b * ns + i, COL_Q // ATTN_WIDTH)),
            pl.BlockSpec((ts, ATTN_WIDTH), lambda b, i: (b * ns + i, COL_K // ATTN_WIDTH)),
            pl.BlockSpec((ts, ATTN_WIDTH), lambda b, i: (b * ns + i, COL_V // ATTN_WIDTH)),
            pl.BlockSpec((AUXT_ROWS, ts), lambda b, i: (0, b * ns + i)),
            pl.BlockSpec((AUXT_ROWS, 1), lambda b, i: (0, 0)),
            pl.BlockSpec((ts, ts), lambda b, i: (0, 0)),
            pl.BlockSpec((LANES, ATTN_HEADS * AUG), lambda b, i: (0, 0)),
        ],
        out_specs=[
            pl.BlockSpec((1, ATTN_HEADS, AUG, ts), lambda b, i: (b, 0, 0, i)),
            pl.BlockSpec((1, ATTN_HEADS, ts, AUG), lambda b, i: (b, 0, i, 0)),
            pl.BlockSpec((1, ATTN_HEADS, 1, VT_ROWS, ts), lambda b, i: (b, 0, i, 0, 0)),
        ],
        out_shape=[
            jax.ShapeDtypeStruct((B, ATTN_HEADS, AUG, S), BF16),
            jax.ShapeDtypeStruct((B, ATTN_HEADS, S, AUG), BF16),
            jax.ShapeDtypeStruct((B, ATTN_HEADS, ns, VT_ROWS, ts), BF16),
        ],
        scratch_shapes=[pltpu.VMEM((AUXT_ROWS, 1), F32)],
        compiler_params=_cparams(("parallel", "arbitrary")),
        name="fox_prep",
    )(proj, proj, proj, auxt, fbt, triu, place)


def _fox_placement():
    rows = jnp.arange(LANES)
    part = rows // ATTN_HEADS
    head = rows % ATTN_HEADS
    cols = jnp.arange(ATTN_HEADS * AUG)
    chead = cols // AUG
    clane = cols % AUG
    valid = (part[:, None] < 3) & (head[:, None] == chead[None, :])
    plus = valid & (clane[None, :] == ATTN_HEAD_DIM + part[:, None])
    minus = valid & (clane[None, :] == ATTN_HEAD_DIM + 3 + part[:, None])
    return (plus.astype(F32) - minus.astype(F32)).astype(BF16)


def _flash_kernel(qt_ref, k_ref, vt_ref, o_ref, m_ref, acc_ref, sa_ref, sb_ref):
    i = pl.program_id(2)
    t = qt_ref.shape[3]
    m_ref[...] = jnp.full_like(m_ref, -jnp.inf)
    acc_ref[...] = jnp.zeros_like(acc_ref)

    def scores(dst_ref, j):
        for h in range(2):
            k = k_ref[0, h, pl.ds(pl.multiple_of(j * t, t), t), :]
            dst_ref[h] = jnp.dot(k, qt_ref[0, h], preferred_element_type=F32)

    def softmax_pv(src_ref, j, masked):
        for h in range(2):
            s = src_ref[h]
            if masked:
                row = lax.broadcasted_iota(jnp.int32, (t, t), 0)
                col = lax.broadcasted_iota(jnp.int32, (t, t), 1)
                s = jnp.where(row <= col, s, -jnp.inf)
            m_prev = m_ref[h]
            m_next = jnp.maximum(m_prev, jnp.max(s, axis=0, keepdims=True))
            alpha = jnp.exp(m_prev - m_next)
            p = jnp.exp(s - m_next).astype(BF16)
            acc_ref[h] = alpha * acc_ref[h] + jnp.dot(vt_ref[0, h, j], p,
                                                      preferred_element_type=F32)
            m_ref[h] = m_next

    last = jnp.maximum(i - 1, 0)
    scores(sa_ref, i)
    scores(sb_ref, 0)
    softmax_pv(sa_ref, i, True)

    def body(jj, carry):
        j0 = 2 * jj
        scores(sa_ref, j0 + 1)
        softmax_pv(sb_ref, j0, False)
        scores(sb_ref, jnp.minimum(j0 + 2, last))
        softmax_pv(sa_ref, j0 + 1, False)
        return carry

    lax.fori_loop(0, i // 2, body, 0)

    @pl.when(i % 2 == 1)
    def _():
        softmax_pv(sb_ref, i - 1, False)

    outs = []
    for h in range(2):
        acc = acc_ref[h]
        outs.append(acc[0:ATTN_HEAD_DIM, :] / acc[ATTN_HEAD_DIM:ATTN_HEAD_DIM + 1, :])
    o_ref[...] = jnp.concatenate(outs, axis=0).T.astype(o_ref.dtype)


def _flash(qt, ka, vt, B, S, t):
    n = S // t
    T = B * S
    hp = ATTN_HEADS // 2
    return pl.pallas_call(
        _flash_kernel,
        grid=(B, hp, n),
        in_specs=[
            pl.BlockSpec((1, 2, AUG, t), lambda b, h, i: (b, h, 0, i)),
            pl.BlockSpec((1, 2, S, AUG), lambda b, h, i: (b, h, 0, 0)),
            pl.BlockSpec((1, 2, n, VT_ROWS, t), lambda b, h, i: (b, h, 0, 0, 0)),
        ],
        out_specs=pl.BlockSpec((t, LANES), lambda b, h, i: (b * n + i, h)),
        out_shape=jax.ShapeDtypeStruct((T, ATTN_WIDTH), BF16),
        scratch_shapes=[
            pltpu.VMEM((2, 1, t), F32),
            pltpu.VMEM((2, VT_ROWS, t), F32),
            pltpu.VMEM((2, t, t), F32),
            pltpu.VMEM((2, t, t), F32),
        ],
        compiler_params=_cparams(("parallel", "parallel", "arbitrary")),
        name="flash",
    )(qt, ka, vt)


PACK_STRIDE = 32
LOG2E = 1.4426950408889634


def _ssd_kernel(zs_ref, x_ref, bc_ref, aux_ref, auxt_ref, dtb_ref, alog_ref,
                dtbt_ref, alogt_ref, dexp_ref, nw_ref, xp_ref,
                y_ref, state_ref):
    ts = zs_ref.shape[0]
    L = SSD_CHUNK

    @pl.when(pl.program_id(1) == 0)
    def _():
        state_ref[...] = jnp.zeros_like(state_ref)

    dt =jax.nn.softplus(aux_ref[...] + dtb_ref[...])
    a = dt * (-jnp.exp(alog_ref[...]))
    dtt = jax.nn.softplus(auxt_ref[...] + dtbt_ref[...])
    at = dtt * (-jnp.exp(alogt_ref[...]))

    row = lax.broadcasted_iota(jnp.int32, (L, L), 0)
    col = lax.broadcasted_iota(jnp.int32, (L, L), 1)
    causal = col <= row
    tril = causal.astype(BF16)
    triu = (row <= col).astype(BF16)
    lane = lax.broadcasted_iota(jnp.int32, (L, LANES), 1)
    first_half = lane < SSD_HEAD_DIM
    xp = xp_ref[...]
    gw = SSD_WIDTH // SSD_GROUPS
    hpg = SSD_HEADS // SSD_GROUPS

    for c in range(ts // L):
        r = slice(c * L, (c + 1) * L)
        a_c = a[r]
        ah, am, al = _split3(a_c)
        a3 = jnp.concatenate([ah, am, al], axis=1).astype(BF16)
        cs3 = jnp.dot(tril, a3, preferred_element_type=F32)
        a_cs = cs3[:, :LANES] + cs3[:, LANES:2 * LANES] + cs3[:, 2 * LANES:]
        th, tm_, tl = _split3(at[:, r])
        t3 = jnp.concatenate([th, tm_, tl], axis=0).astype(BF16)
        cst = jnp.dot(t3, triu, preferred_element_type=F32)
        a_cst = cst[0:AUXT_ROWS] + cst[AUXT_ROWS:2 * AUXT_ROWS] + cst[2 * AUXT_ROWS:]

        a_l2 = a_cs * LOG2E
        a_l2t = a_cst * LOG2E
        e_small = jnp.exp(a_cs)
        w_small = jnp.exp(a_cs[L - 1:L, :] - a_cs) * dt[r]
        expand = lambda v: jnp.dot(_pack3(v, lane, PACK_STRIDE), xp, preferred_element_type=F32)
        dt_exp = expand(dt[r])
        w_exp = expand(w_small)
        e_exp = expand(e_small)

        x_c = x_ref[r, :].astype(F32)
        xd = (x_c * dt_exp).astype(BF16)
        xw = (x_c * w_exp).astype(BF16)
        y_parts = []
        for g in range(SSD_GROUPS):
            b_gb = bc_ref[r, g * SSD_STATE:(g + 1) * SSD_STATE]
            c_gb = bc_ref[r, (SSD_GROUPS + g) * SSD_STATE:(SSD_GROUPS + g + 1) * SSD_STATE]
            cb = lax.dot_general(c_gb, b_gb, (((1,), (1,)), ((), ())),
                                 preferred_element_type=F32)
            gcols = slice(g * gw, (g + 1) * gw)
            st = state_ref[:, gcols]
            y_off = jnp.dot(c_gb, st.astype(BF16), preferred_element_type=F32) * e_exp[:, gcols]
            state_ref[:, gcols] = e_exp[L - 1:L, gcols] * st + jnp.dot(
                b_gb.astype(F32).T.astype(BF16), xw[:, gcols], preferred_element_type=F32)
            for pr in range(hpg // 2):
                ms = []
                for e in range(2):
                    h = g * hpg + pr * 2 + e
                    seg = a_l2[:, AUX_DT0 + h:AUX_DT0 + h + 1] - a_l2t[AUX_DT0 + h:AUX_DT0 + h + 1, :]
                    lm = jnp.exp2(jnp.where(causal, seg, -jnp.inf))
                    ms.append((cb * lm).astype(BF16))
                lhs = jnp.concatenate(ms, axis=1)
                p0 = g * gw + pr * LANES
                xpair = xd[:, p0:p0 + LANES]
                zero = jnp.zeros_like(xpair)
                rhs = jnp.concatenate([jnp.where(first_half, xpair, zero),
                                       jnp.where(first_half, zero, xpair)], axis=0)
                y_parts.append(jnp.dot(lhs, rhs, preferred_element_type=F32)
                               + y_off[:, pr * LANES:(pr + 1) * LANES])
        y = jnp.concatenate(y_parts, axis=1) + dexp_ref[...] * x_c
        y = y * zs_ref[r, :].astype(F32)
        outs = []
        for g in range(SSD_GROUPS):
            yg = y[:, g * gw:(g + 1) * gw]
            yg = yg * lax.rsqrt(jnp.mean(yg * yg, axis=-1, keepdims=True) + NORM_EPS)
            outs.append(yg)
        y_ref[r, :] = (jnp.concatenate(outs, axis=1) * nw_ref[...]).astype(y_ref.dtype)


def _ssd(proj, aux, auxt, dtb128, alog128, dtbt, alogt, dexp, nw, xp, B, S, ts=512):
    ns = S // ts
    T = B * S
    return pl.pallas_call(
        _ssd_kernel,
        grid=(B, ns),
        in_specs=[
            pl.BlockSpec((ts, SSD_WIDTH), lambda b, i: (b * ns + i, COL_Z // SSD_WIDTH)),
            pl.BlockSpec((ts, SSD_WIDTH), lambda b, i: (b * ns + i, COL_X // SSD_WIDTH)),
            pl.BlockSpec((ts, BC_WIDTH), lambda b, i: (b * ns + i, COL_BC // BC_WIDTH)),
            pl.BlockSpec((ts, LANES), lambda b, i: (b * ns + i, 0)),
            pl.BlockSpec((AUXT_ROWS, ts), lambda b, i: (0, b * ns + i)),
            pl.BlockSpec((1, LANES), lambda b, i: (0, 0)),
            pl.BlockSpec((1, LANES), lambda b, i: (0, 0)),
            pl.BlockSpec((AUXT_ROWS, 1), lambda b, i: (0, 0)),
            pl.BlockSpec((AUXT_ROWS, 1), lambda b, i: (0, 0)),
            pl.BlockSpec((1, SSD_WIDTH), lambda b, i: (0, 0)),
            pl.BlockSpec((1, SSD_WIDTH), lambda b, i: (0, 0)),
            pl.BlockSpec((LANES, SSD_WIDTH), lambda b, i: (0, 0)),
        ],
        out_specs=pl.BlockSpec((ts, SSD_WIDTH), lambda b, i: (b * ns + i, 0)),
        out_shape=jax.ShapeDtypeStruct((T, SSD_WIDTH), BF16),
        scratch_shapes=[pltpu.VMEM((SSD_STATE, SSD_WIDTH), F32)],
        compiler_params=_cparams(("parallel", "arbitrary")),
        name="ssd",
    )(proj, proj, proj, aux, auxt, dtb128, alog128, dtbt, alogt, dexp, nw, xp)


def _ssd_expand_matrix():
    rows = jnp.arange(LANES)
    slot = rows % PACK_STRIDE
    part = rows // PACK_STRIDE
    head = slot - AUX_DT0
    cols = jnp.arange(SSD_WIDTH) // SSD_HEAD_DIM
    valid = (part < 3) & (head >= 0) & (head < SSD_HEADS)
    return jnp.where(valid[:, None] & (head[:, None] == cols[None, :]), 1.0, 0.0).astype(BF16)


POOL_HALO = 16


def _merge_kernel(x_ref, gates_ref, pool_ref, prev_ref, attn_ref, ssd_ref, mixw_ref, pscale_ref,
                  ppool_ref, pattn_ref, pssd_ref, wout_ref, o_ref):
    tm = x_ref.shape[0]
    i = pl.program_id(1)
    gdim = POOL_WIDTH // len(POOL_WINDOWS)

    prev = prev_ref[...].astype(F32)
    prev = jnp.where(i == 0, jnp.zeros_like(prev), prev)
    head_pos = i * tm + lax.broadcasted_iota(jnp.int32, (POOL_HALO, gdim), 0)
    ys = []
    for g, w in enumerate(POOL_WINDOWS):
        assert w <= POOL_HALO and w & (w - 1) == 0
        cols = slice(g * gdim, (g + 1) * gdim)
        v = pool_ref[:, cols].astype(F32)
        acc = jnp.concatenate([prev[:, cols], v], axis=0)
        span = 1
        while span < w:
            acc = acc + pltpu.roll(acc, span, 0)
            span *= 2
        inv_cnt = jnp.concatenate(
            [1.0 / jnp.minimum(head_pos + 1, w).astype(F32),
             jnp.full((tm - POOL_HALO, gdim), 1.0 / w, F32)], axis=0)
        d = (acc[POOL_HALO:, :] * inv_cnt - v).astype(BF16)
        ys.append(jnp.dot(d, mixw_ref[g], preferred_element_type=F32))
    ypre = (jnp.concatenate(ys, axis=1) * pscale_ref[...]).astype(BF16)
    y_pool = jnp.dot(ypre, ppool_ref[...], preferred_element_type=F32)
    y_attn = jnp.dot(attn_ref[...], pattn_ref[...], preferred_element_type=F32)
    y_ssd = jnp.dot(ssd_ref[...], pssd_ref[...], preferred_element_type=F32)
    def gate(k):
        x = gates_ref[:, k * D_MODEL:(k + 1) * D_MODEL].astype(F32)
        return 0.5 * jnp.tanh(0.5 * x) + 0.5

    merged = gate(0) * y_pool + gate(1) * y_attn + gate(2) * y_ssd
    o_ref[...] = x_ref[...] + jnp.dot(merged.astype(BF16), wout_ref[...],
                                      preferred_element_type=F32)


def _merge(x2d, proj, attn, yssd, mixw, pscale, ppool, pattn, pssd, wout, layer, B, S, tm=512):
    ns = S // tm
    T = B * S
    hb = tm // POOL_HALO
    const2 = lambda b, i: (0, 0)
    stacked = lambda *shape: pl.BlockSpec((None,) + shape, lambda b, i: (layer,) + (0,) * len(shape))
    return pl.pallas_call(
        _merge_kernel,
        grid=(B, ns),
        in_specs=[
            pl.BlockSpec((tm, D_MODEL), lambda b, i: (b * ns + i, 0)),
            pl.BlockSpec((tm, 3 * D_MODEL), lambda b, i: (b * ns + i, 0)),
            pl.BlockSpec((tm, POOL_WIDTH), lambda b, i: (b * ns + i, COL_POOL // POOL_WIDTH)),
            pl.BlockSpec((POOL_HALO, POOL_WIDTH),
                         lambda b, i: (jnp.maximum((b * ns + i) * hb - 1, 0), COL_POOL // POOL_WIDTH)),
            pl.BlockSpec((tm, ATTN_WIDTH), lambda b, i: (b * ns + i, 0)),
            pl.BlockSpec((tm, SSD_WIDTH), lambda b, i: (b * ns + i, 0)),
            stacked(len(POOL_WINDOWS), LANES, LANES),
            pl.BlockSpec((1, POOL_WIDTH), const2),
            stacked(POOL_WIDTH, D_MODEL),
            stacked(ATTN_WIDTH, D_MODEL),
            stacked(SSD_WIDTH, D_MODEL),
            stacked(D_MODEL, D_MODEL),
        ],
        out_specs=pl.BlockSpec((tm, D_MODEL), lambda b, i: (b * ns + i, 0)),
        out_shape=jax.ShapeDtypeStruct((T, D_MODEL), F32),
        compiler_params=_cparams(("parallel", "parallel")),
        name="merge",
    )(x2d, proj, proj, proj, attn, yssd, mixw, pscale, ppool, pattn, pssd, wout)


FFN_CHUNK = 256


def _ffn_kernel(x_ref, nw_ref, wup_ref, cw_ref, cb_ref, wd_ref, fnw_ref,
                o_ref, u_ref, act_ref, stage_ref, car_ref, *, final_norm):
    tm = x_ref.shape[0]
    halo = SUBLANES
    nch = FFN_DIM // FFN_CHUNK

    x = x_ref[...]
    ms = jnp.mean(x * x, axis=-1, keepdims=True)
    u_ref[...] = (x * lax.rsqrt(ms + NORM_EPS) * nw_ref[...]).astype(BF16)

    @pl.when(pl.program_id(1) == 0)
    def _():
        car_ref[...] = jnp.zeros_like(car_ref)

    def cols_of(c, half):
        start = half * FFN_DIM + c * FFN_CHUNK
        return slice(start, start + FFN_CHUNK)

    def up(c):
        for half in range(2):
            stage_ref[2 * (c % 2) + half] = jnp.dot(u_ref[...], wup_ref[:, cols_of(c, half)],
                                                    preferred_element_type=F32)

    def conv(c, half):
        cols = cols_of(c, half)
        h = stage_ref[2 * (c % 2) + half]
        ext = jnp.concatenate([car_ref[:, cols], h], axis=0)
        car_ref[:, cols] = h[tm - halo:tm, :]
        out = (cw_ref[2:3, cols] * ext + cw_ref[1:2, cols] * pltpu.roll(ext, 1, 0)
               + cw_ref[0:1, cols] * pltpu.roll(ext, 2, 0))
        return out[halo:, :] + cb_ref[:, cols]

    up(0)
    for c in range(nch):
        if c + 1 < nch:
            up(c + 1)
        act_ref[:, c * FFN_CHUNK:(c + 1) * FFN_CHUNK] = (_silu(conv(c, 0)) * conv(c, 1)).astype(BF16)

    y = x_ref[...] + jnp.dot(act_ref[...], wd_ref[...], preferred_element_type=F32)
    if final_norm:
        ms = jnp.mean(y * y, axis=-1, keepdims=True)
        y = y * lax.rsqrt(ms + NORM_EPS) * fnw_ref[...]
    o_ref[...] = y


def _ffn(x2d, nw, wup, cw, cb, wdown, fnw, layer, B, S, final_norm, tm=512):
    ns = S // tm
    T = B * S
    whole = lambda shape: pl.BlockSpec(shape, lambda b, i: (0, 0))
    stacked = lambda *shape: pl.BlockSpec((None,) + shape, lambda b, i: (layer, 0, 0))
    return pl.pallas_call(
        functools.partial(_ffn_kernel, final_norm=final_norm),
        grid=(B, ns),
        in_specs=[
            pl.BlockSpec((tm, D_MODEL), lambda b, i: (b * ns + i, 0)),
            whole((1, D_MODEL)),
            stacked(D_MODEL, 2 * FFN_DIM),
            whole((FFN_CONV, 2 * FFN_DIM)),
            whole((1, 2 * FFN_DIM)),
            stacked(FFN_DIM, D_MODEL),
            whole((1, D_MODEL)),
        ],
        out_specs=pl.BlockSpec((tm, D_MODEL), lambda b, i: (b * ns + i, 0)),
        out_shape=jax.ShapeDtypeStruct((T, D_MODEL), F32),
        scratch_shapes=[
            pltpu.VMEM((tm, D_MODEL), BF16),
            pltpu.VMEM((tm, FFN_DIM), BF16),
            pltpu.VMEM((4, tm, FFN_CHUNK), F32),
            pltpu.VMEM((SUBLANES, 2 * FFN_DIM), F32),
        ],
        compiler_params=_cparams(("parallel", "arbitrary")),
        name="ffn",
    )(x2d, nw, wup, cw, cb, wdown, fnw)


def _pad_lanes(v, offset, width=LANES):
    return jnp.zeros((1, width), F32).at[0, offset:offset + v.shape[0]].set(v.astype(F32))


IN_POOL, IN_QKV, IN_F, IN_Z, IN_X, IN_BC, IN_DT, IN_GATES = 0, 512, 2048, 2056, 3080, 4104, 4616, 4632
IN_TOTAL = IN_GATES + 3 * D_MODEL


WPREP_CHUNK = 512
WPREP_SRC = tuple(start + WPREP_CHUNK * k
                  for start, width in ((IN_GATES, 3 * D_MODEL), (IN_Z, SSD_WIDTH), (IN_BC, BC_WIDTH),
                                       (IN_POOL, POOL_WIDTH), (IN_X, SSD_WIDTH),
                                       (IN_QKV, 3 * ATTN_WIDTH))
                  for k in range(width // WPREP_CHUNK))
assert len(WPREP_SRC) * WPREP_CHUNK == PROJ_COLS and all(s % SUBLANES == 0 for s in WPREP_SRC)


def _wprep_kernel(src_ref, wt_ref, wp_ref):
    del src_ref
    wp_ref[...] = wt_ref[0].T.astype(BF16)


def _wprep(w_in):
    L, K, N = w_in.shape
    assert N == IN_TOTAL
    wt = jnp.swapaxes(w_in, 1, 2)
    w_perm = pl.pallas_call(
        _wprep_kernel,
        grid_spec=pltpu.PrefetchScalarGridSpec(
            num_scalar_prefetch=1,
            grid=(L, len(WPREP_SRC)),
            in_specs=[pl.BlockSpec((pl.Element(1), pl.Element(WPREP_CHUNK), pl.Element(K)),
                                   lambda l, c, src: (l, pl.multiple_of(src[c], SUBLANES), 0))],
            out_specs=pl.BlockSpec((None, K, WPREP_CHUNK), lambda l, c, src: (l, 0, c)),
        ),
        out_shape=jax.ShapeDtypeStruct((L, K, PROJ_COLS), BF16),
        compiler_params=_cparams(("parallel", "parallel")),
        name="wprep",
    )(jnp.asarray(WPREP_SRC, jnp.int32), wt)
    small = jnp.concatenate([wt[:, IN_F:IN_F + ATTN_HEADS], wt[:, IN_DT:IN_DT + SSD_HEADS]], axis=1)
    wauxt = jnp.pad(small, ((0, 0), (0, AUXT_ROWS - small.shape[1]), (0, 0))).astype(BF16)
    return w_perm, wauxt


def kernel(x, norm_mix, w_in, pool_mix, pool_scale, f_bias, ssd_conv_w, ssd_conv_b, ssd_dt_bias,
           ssd_a_log, ssd_d, ssd_norm, p_pool, p_attn, p_ssd, w_out, norm_ffn, ffn_up, ffn_conv_w,
           ffn_conv_b, ffn_down, norm_final):
    B, S, D = x.shape
    depth = w_in.shape[0]
    T = B * S
    x2d = x.reshape(T, D)
    place = _fox_placement()
    xp = _ssd_expand_matrix()
    fnw = norm_final.reshape(1, D)
    w_perm, wauxt = _wprep(w_in)
    mixw, ppool, pattn, pssd, wout, wup, wdown = (
        w.astype(BF16) for w in (pool_mix, p_pool, p_attn, p_ssd, w_out, ffn_up, ffn_down))
    for l in range(depth):
        proj, aux, auxt = _in_proj(x2d, norm_mix[l].reshape(1, D), w_perm, wauxt, l,
                                   ssd_conv_w[l], ssd_conv_b[l].reshape(1, -1), S)
        qt, ka, vt = _fox_prep(proj, auxt, _pad_lanes(f_bias[l], AUX_F0, AUXT_ROWS).T, place, B, S,
                               ATTN_TILE)
        attn = _flash(qt, ka, vt, B, S, ATTN_TILE)
        dtb128 = _pad_lanes(ssd_dt_bias[l], AUX_DT0)
        alog128 = _pad_lanes(ssd_a_log[l], AUX_DT0)
        yssd = _ssd(proj, aux, auxt, dtb128, alog128,
                    _pad_lanes(ssd_dt_bias[l], AUX_DT0, AUXT_ROWS).T,
                    _pad_lanes(ssd_a_log[l], AUX_DT0, AUXT_ROWS).T,
                    jnp.repeat(ssd_d[l], SSD_HEAD_DIM).reshape(1, -1), ssd_norm[l].reshape(1, -1),
                    xp, B, S)
        x2d = _merge(x2d, proj, attn, yssd, mixw, pool_scale[l].reshape(1, -1),
                     ppool, pattn, pssd, wout, l, B, S)
        x2d = _ffn(x2d, norm_ffn[l].reshape(1, D), wup, ffn_conv_w[l],
                   ffn_conv_b[l].reshape(1, -1), wdown, fnw, l, B, S,
                   final_norm=(l == depth - 1))
    return x2d.reshape(B, S, D)
```

```python
import functools

import jax
import jax.numpy as jnp
from jax import lax
from jax.experimental import pallas as pl
from jax.experimental.pallas import tpu as pltpu

F32 = jnp.float32
BF16 = jnp.bfloat16

D_MODEL = 1024
NORM_EPS = 1e-6
POOL_WINDOWS = (2, 4, 8, 16)
POOL_WIDTH = 512
ATTN_HEADS = 8
ATTN_HEAD_DIM = 64
ATTN_WIDTH = 512
SSD_HEADS = 16
SSD_HEAD_DIM = 64
SSD_WIDTH = 1024
SSD_GROUPS = 2
SSD_STATE = 128
SSD_CONV = 4
SSD_CHUNK = 128
SSD_CONV_CH = 1536
FFN_DIM = 2816
FFN_CONV = 3

LANES = 128
SUBLANES = 8

COL_GATES = 0
COL_Z = 3072
COL_BC = 4096
COL_POOL = 4608
COL_X = 5120
COL_Q = 6144
COL_K = 6656
COL_V = 7168
PROJ_COLS = 7680
BC_WIDTH = 2 * SSD_GROUPS * SSD_STATE
PROJ_TILE = 1536
PROJ_CHUNK = 256
TILE_ZBC = COL_Z // PROJ_TILE
TILE_POOLX = COL_POOL // PROJ_TILE
assert COL_Z == TILE_ZBC * PROJ_TILE and COL_BC == COL_Z + SSD_WIDTH
assert COL_POOL == TILE_POOLX * PROJ_TILE and COL_X == COL_POOL + POOL_WIDTH
AUX_F0 = 0
AUX_DT0 = 8
AUXT_ROWS = 32

VMEM_LIMIT = 56 * 1024 * 1024


def _cparams(sem):
    return pltpu.CompilerParams(dimension_semantics=sem, vmem_limit_bytes=VMEM_LIMIT)


def _split3(x):
    hi = x.astype(BF16).astype(F32)
    r1 = x - hi
    mid = r1.astype(BF16).astype(F32)
    lo = (r1 - mid).astype(BF16).astype(F32)
    return hi, mid, lo


def _pack3(x, lane, stride):
    hi, mid, lo = _split3(x)
    packed = jnp.where(lane < stride, hi,
                       jnp.where(lane < 2 * stride, pltpu.roll(mid, stride, 1),
                                 pltpu.roll(lo, 2 * stride, 1)))
    return packed.astype(BF16)


def _silu(x):
    h = 0.5 * x
    return h + h * jnp.tanh(h)


def _inproj_kernel(x_ref, nw_ref, w_ref, wauxt_ref, cw_ref, cb_ref,
                   out_ref, aux_ref, auxt_ref, u_ref, halo_ref, acc_ref, *, tiles_per_seq):
    i = pl.program_id(0)
    j = pl.program_id(1)
    tm = x_ref.shape[0]
    halo = SUBLANES

    def conv_silu(acc, wcols):
        prev = halo_ref[:, wcols]
        prev = jnp.where(i % tiles_per_seq == 0, jnp.zeros_like(prev), prev)
        ext = jnp.concatenate([prev, acc], axis=0)
        halo_ref[:, wcols] = acc[tm - halo:tm, :]
        ext1 = pltpu.roll(ext, 1, 0)
        near = cw_ref[3:4, wcols] * ext + cw_ref[2:3, wcols] * ext1
        far = cw_ref[1:2, wcols] * ext + cw_ref[0:1, wcols] * ext1
        conv = (near + pltpu.roll(far, 2, 0))[halo:, :] + cb_ref[:, wcols]
        return _silu(conv)

    def tile(epilogues, lhs=None):
        def chunk_dot(c):
            acc_ref[c % 2] = jnp.dot(u_ref[...] if lhs is None else lhs,
                                     w_ref[:, c * PROJ_CHUNK:(c + 1) * PROJ_CHUNK],
                                     preferred_element_type=F32)

        chunk_dot(0)
        for c, epilogue in enumerate(epilogues):
            if c + 1 < len(epilogues):
                chunk_dot(c + 1)
            out_ref[:, c * PROJ_CHUNK:(c + 1) * PROJ_CHUNK] = epilogue(
                acc_ref[c % 2]).astype(out_ref.dtype)

    def conv_at(off):
        return lambda acc: conv_silu(acc, slice(off, off + PROJ_CHUNK))

    plain = lambda acc: acc
    nchunk = out_ref.shape[1] // PROJ_CHUNK

    @pl.when(j == 0)
    def _():
        x = x_ref[...]
        ms = jnp.mean(x * x, axis=-1, keepdims=True)
        u = (x * lax.rsqrt(ms + NORM_EPS) * nw_ref[...]).astype(BF16)
        u_ref[...] = u
        auxt = lax.dot_general(wauxt_ref[...], u, (((1,), (1,)), ((), ())),
                               preferred_element_type=F32)
        auxt_ref[...] = auxt
        aux_ref[...] = jnp.concatenate(
            [auxt, jnp.zeros((LANES - AUXT_ROWS, tm), F32)], axis=0).T
        tile([plain] * nchunk, lhs=u)

    @pl.when((j != 0) & (j != TILE_ZBC) & (j != TILE_POOLX))
    def _():
        tile([plain] * nchunk)

    def convs(first, width):
        return [conv_at(first + k * PROJ_CHUNK) for k in range(width // PROJ_CHUNK)]

    @pl.when(j == TILE_ZBC)
    def _():
        tile([_silu] * (SSD_WIDTH // PROJ_CHUNK) + convs(SSD_WIDTH, BC_WIDTH))

    @pl.when(j == TILE_POOLX)
    def _():
        tile([plain] * (POOL_WIDTH // PROJ_CHUNK) + convs(0, SSD_WIDTH))


def _in_proj(x2d, norm_w, w_perm, wauxt, layer, cw, cb, S, tm=1024, tn=PROJ_TILE):
    T = x2d.shape[0]
    return pl.pallas_call(
        functools.partial(_inproj_kernel, tiles_per_seq=S // tm),
        grid=(T // tm, PROJ_COLS // tn),
        in_specs=[
            pl.BlockSpec((tm, D_MODEL), lambda i, j: (i, 0)),
            pl.BlockSpec((1, D_MODEL), lambda i, j: (0, 0)),
            pl.BlockSpec((None, D_MODEL, tn), lambda i, j: (layer, 0, j)),
            pl.BlockSpec((None, AUXT_ROWS, D_MODEL), lambda i, j: (layer, 0, 0)),
            pl.BlockSpec((SSD_CONV, SSD_CONV_CH), lambda i, j: (0, 0)),
            pl.BlockSpec((1, SSD_CONV_CH), lambda i, j: (0, 0)),
        ],
        out_specs=[
            pl.BlockSpec((tm, tn), lambda i, j: (i, j)),
            pl.BlockSpec((tm, LANES), lambda i, j: (i, 0)),
            pl.BlockSpec((AUXT_ROWS, tm), lambda i, j: (0, i)),
        ],
        out_shape=[
            jax.ShapeDtypeStruct((T, PROJ_COLS), BF16),
            jax.ShapeDtypeStruct((T, LANES), F32),
            jax.ShapeDtypeStruct((AUXT_ROWS, T), F32),
        ],
        scratch_shapes=[pltpu.VMEM((tm, D_MODEL), BF16),
                        pltpu.VMEM((SUBLANES, SSD_CONV_CH), F32),
                        pltpu.VMEM((2, tm, PROJ_CHUNK), F32)],
        compiler_params=_cparams(("arbitrary", "arbitrary")),
        name="in_proj",
    )(x2d, norm_w, w_perm, wauxt, cw, cb)


AUG = 2 * ATTN_HEAD_DIM
ATTN_TILE = 512
VT_ROWS = ATTN_HEAD_DIM + 16


def _foxprep_kernel(q_ref, k_ref, v_ref, auxt_ref, fbt_ref, triu_ref, place_ref,
                    qt_ref, ka_ref, vt_ref, carry_ref):
    ts = q_ref.shape[0]
    nh = ATTN_HEADS
    R = AUXT_ROWS

    @pl.when(pl.program_id(1) == 0)
    def _():
        carry_ref[...] = jnp.zeros_like(carry_ref)

    logf = jax.nn.log_sigmoid(auxt_ref[...] + fbt_ref[...])
    parts = jnp.concatenate(_split3(logf), axis=0).astype(BF16)
    cs = jnp.dot(parts, triu_ref[...], preferred_element_type=F32)
    ct = cs[0:R] + cs[R:2 * R] + cs[2 * R:3 * R] + carry_ref[...]
    carry_ref[...] = ct[:, ts - 1:ts]

    ch, cm, cl = _split3(ct)
    stacked = jnp.concatenate([ch[0:nh], cm[0:nh], cl[0:nh],
                               jnp.zeros((LANES - 3 * nh, ts), F32)], axis=0)
    aug = jnp.dot(stacked.T.astype(BF16), place_ref[...], preferred_element_type=F32)
    lane = lax.broadcasted_iota(jnp.int32, (ts, LANES), 1)
    is_q = lane < ATTN_HEAD_DIM
    is_pos = lane < ATTN_HEAD_DIM + 3
    ones_hi = jnp.where((lane >= ATTN_HEAD_DIM + 3) & (lane < ATTN_HEAD_DIM + 6), 1.0, 0.0)
    ones_lo = jnp.where((lane >= ATTN_HEAD_DIM) & is_pos, 1.0, 0.0)
    scale = ATTN_HEAD_DIM ** -0.5
    row_id = lax.broadcasted_iota(jnp.int32, (VT_ROWS - ATTN_HEAD_DIM, ts), 0)
    ones_rows = jnp.where(row_id == 0, 1.0, 0.0)
    for h in range(ATTN_HEADS):
        pair = slice((h // 2) * LANES, (h // 2 + 1) * LANES)
        blk = slice(h * LANES, (h + 1) * LANES)
        qp = q_ref[:, pair].astype(F32)
        kp = k_ref[:, pair].astype(F32)
        if h % 2 == 1:
            qp = pltpu.roll(qp, ATTN_HEAD_DIM, 1)
            kp = pltpu.roll(kp, ATTN_HEAD_DIM, 1)
        qa = jnp.where(is_q, qp * scale, jnp.where(is_pos, aug[:, blk], ones_hi))
        qt_ref[0, h] = qa.T.astype(BF16)
        ka_ref[0, h] = jnp.where(is_q, kp, jnp.where(is_pos, ones_lo, aug[:, blk])).astype(BF16)
        vt = v_ref[:, pair].astype(F32).T
        vh = vt[(h % 2) * ATTN_HEAD_DIM:(h % 2 + 1) * ATTN_HEAD_DIM, :]
        vt_ref[0, h, 0] = jnp.concatenate([vh, ones_rows], axis=0).astype(BF16)


def _fox_prep(proj, auxt, fbt, place, B, S, ts):
    ns = S // ts
    triu = (jnp.arange(ts)[:, None] <= jnp.arange(ts)[None, :]).astype(BF16)
    return pl.pallas_call(
        _foxprep_kernel,
        grid=(B, ns),
        in_specs=[
            pl.BlockSpec((ts, ATTN_WIDTH), lambda b, i: (b * ns + i, COL_Q // ATTN_WIDTH)),
            pl.BlockSpec((ts, ATTN_WIDTH), lambda b, i: (b * ns + i, COL_K // ATTN_WIDTH)),
            pl.BlockSpec((ts, ATTN_WIDTH), lambda b, i: (b * ns + i, COL_V // ATTN_WIDTH)),
            pl.BlockSpec((AUXT_ROWS, ts), lambda b, i: (0, b * ns + i)),
            pl.BlockSpec((AUXT_ROWS, 1), lambda b, i: (0, 0)),
            pl.BlockSpec((ts, ts), lambda b, i: (0, 0)),
            pl.BlockSpec((LANES, ATTN_HEADS * AUG), lambda b, i: (0, 0)),
        ],
        out_specs=[
            pl.BlockSpec((1, ATTN_HEADS, AUG, ts), lambda b, i: (b, 0, 0, i)),
            pl.BlockSpec((1, ATTN_HEADS, ts, AUG), lambda b, i: (b, 0, i, 0)),
            pl.BlockSpec((1, ATTN_HEADS, 1, VT_ROWS, ts), lambda b, i: (b, 0, i, 0, 0)),
        ],
        out_shape=[
            jax.ShapeDtypeStruct((B, ATTN_HEADS, AUG, S), BF16),
            jax.ShapeDtypeStruct((B, ATTN_HEADS, S, AUG), BF16),
            jax.ShapeDtypeStruct((B, ATTN_HEADS, ns, VT_ROWS, ts), BF16),
        ],
        scratch_shapes=[pltpu.VMEM((AUXT_ROWS, 1), F32)],
        compiler_params=_cparams(("parallel", "arbitrary")),
        name="fox_prep",
    )(proj, proj, proj, auxt, fbt, triu, place)


def _fox_placement():
    rows = jnp.arange(LANES)
    part = rows // ATTN_HEADS
    head = rows % ATTN_HEADS
    cols = jnp.arange(ATTN_HEADS * AUG)
    chead = cols // AUG
    clane = cols % AUG
    valid = (part[:, None] < 3) & (head[:, None] == chead[None, :])
    plus = valid & (clane[None, :] == ATTN_HEAD_DIM + part[:, None])
    minus = valid & (clane[None, :] == ATTN_HEAD_DIM + 3 + part[:, None])
    return (plus.astype(F32) - minus.astype(F32)).astype(BF16)


def _flash_kernel(qt_ref, k_ref, vt_ref, o_ref, m_ref, acc_ref, sa_ref, sb_ref):
    i = pl.program_id(2)
    t = qt_ref.shape[3]
    m_ref[...] = jnp.full_like(m_ref, -jnp.inf)
    acc_ref[...] = jnp.zeros_like(acc_ref)

    def scores(dst_ref, j):
        for h in range(2):
            k = k_ref[0, h, pl.ds(pl.multiple_of(j * t, t), t), :]
            dst_ref[h] = jnp.dot(k, qt_ref[0, h], preferred_element_type=F32)

    def softmax_pv(src_ref, j, masked):
        for h in range(2):
            s = src_ref[h]
            if masked:
                row = lax.broadcasted_iota(jnp.int32, (t, t), 0)
                col = lax.broadcasted_iota(jnp.int32, (t, t), 1)
                s = jnp.where(row <= col, s, -jnp.inf)
            m_prev = m_ref[h]
            m_next = jnp.maximum(m_prev, jnp.max(s, axis=0, keepdims=True))
            alpha = jnp.exp(m_prev - m_next)
            p = jnp.exp(s - m_next).astype(BF16)
            acc_ref[h] = alpha * acc_ref[h] + jnp.dot(vt_ref[0, h, j], p,
                                                      preferred_element_type=F32)
            m_ref[h] = m_next

    last = jnp.maximum(i - 1, 0)
    scores(sa_ref, i)
    scores(sb_ref, 0)
    softmax_pv(sa_ref, i, True)

    def body(jj, carry):
        j0 = 2 * jj
        scores(sa_ref, j0 + 1)
        softmax_pv(sb_ref, j0, False)
        scores(sb_ref, jnp.minimum(j0 + 2, last))
        softmax_pv(sa_ref, j0 + 1, False)
        return carry

    lax.fori_loop(0, i // 2, body, 0)

    @pl.when(i % 2 == 1)
    def _():
        softmax_pv(sb_ref, i - 1, False)

    outs = []
    for h in range(2):
        acc = acc_ref[h]
        outs.append(acc[0:ATTN_HEAD_DIM, :] / acc[ATTN_HEAD_DIM:ATTN_HEAD_DIM + 1, :])
    o_ref[...] = jnp.concatenate(outs, axis=0).T.astype(o_ref.dtype)


def _flash(qt, ka, vt, B, S, t):
    n = S // t
    T = B * S
    hp = ATTN_HEADS // 2
    return pl.pallas_call(
        _flash_kernel,
        grid=(B, hp, n),
        in_specs=[
            pl.BlockSpec((1, 2, AUG, t), lambda b, h, i: (b, h, 0, i)),
            pl.BlockSpec((1, 2, S, AUG), lambda b, h, i: (b, h, 0, 0)),
            pl.BlockSpec((1, 2, n, VT_ROWS, t), lambda b, h, i: (b, h, 0, 0, 0)),
        ],
        out_specs=pl.BlockSpec((t, LANES), lambda b, h, i: (b * n + i, h)),
        out_shape=jax.ShapeDtypeStruct((T, ATTN_WIDTH), BF16),
        scratch_shapes=[
            pltpu.VMEM((2, 1, t), F32),
            pltpu.VMEM((2, VT_ROWS, t), F32),
            pltpu.VMEM((2, t, t), F32),
            pltpu.VMEM((2, t, t), F32),
        ],
        compiler_params=_cparams(("parallel", "parallel", "arbitrary")),
        name="flash",
    )(qt, ka, vt)


PACK_STRIDE = 32
LOG2E = 1.4426950408889634


def _ssd_kernel(zs_ref, x_ref, bc_ref, aux_ref, auxt_ref, dtb_ref, alog_ref,
                dtbt_ref, alogt_ref, dexp_ref, nw_ref, xp_ref,
                y_ref, state_ref):
    ts = zs_ref.shape[0]
    L = SSD_CHUNK

    @pl.when(pl.program_id(1) == 0)
    def _():
        state_ref[...] = jnp.zeros_like(state_ref)

    dt =jax.nn.softplus(aux_ref[...] + dtb_ref[...])
    a = dt * (-jnp.exp(alog_ref[...]))
    dtt = jax.nn.softplus(auxt_ref[...] + dtbt_ref[...])
    at = dtt * (-jnp.exp(alogt_ref[...]))

    row = lax.broadcasted_iota(jnp.int32, (L, L), 0)
    col = lax.broadcasted_iota(jnp.int32, (L, L), 1)
    causal = col <= row
    tril = causal.astype(BF16)
    triu = (row <= col).astype(BF16)
    lane = lax.broadcasted_iota(jnp.int32, (L, LANES), 1)
    first_half = lane < SSD_HEAD_DIM
    xp = xp_ref[...]
    gw = SSD_WIDTH // SSD_GROUPS
    hpg = SSD_HEADS // SSD_GROUPS

    for c in range(ts // L):
        r = slice(c * L, (c + 1) * L)
        a_c = a[r]
        ah, am, al = _split3(a_c)
        a3 = jnp.concatenate([ah, am, al], axis=1).astype(BF16)
        cs3 = jnp.dot(tril, a3, preferred_element_type=F32)
        a_cs = cs3[:, :LANES] + cs3[:, LANES:2 * LANES] + cs3[:, 2 * LANES:]
        th, tm_, tl = _split3(at[:, r])
        t3 = jnp.concatenate([th, tm_, tl], axis=0).astype(BF16)
        cst = jnp.dot(t3, triu, preferred_element_type=F32)
        a_cst = cst[0:AUXT_ROWS] + cst[AUXT_ROWS:2 * AUXT_ROWS] + cst[2 * AUXT_ROWS:]

        a_l2 = a_cs * LOG2E
        a_l2t = a_cst * LOG2E
        e_small = jnp.exp(a_cs)
        w_small = jnp.exp(a_cs[L - 1:L, :] - a_cs) * dt[r]
        expand = lambda v: jnp.dot(_pack3(v, lane, PACK_STRIDE), xp, preferred_element_type=F32)
        dt_exp = expand(dt[r])
        w_exp = expand(w_small)
        e_exp = expand(e_small)

        x_c = x_ref[r, :].astype(F32)
        xd = (x_c * dt_exp).astype(BF16)
        xw = (x_c * w_exp).astype(BF16)
        y_parts = []
        for g in range(SSD_GROUPS):
            b_gb = bc_ref[r, g * SSD_STATE:(g + 1) * SSD_STATE]
            c_gb = bc_ref[r, (SSD_GROUPS + g) * SSD_STATE:(SSD_GROUPS + g + 1) * SSD_STATE]
            cb = lax.dot_general(c_gb, b_gb, (((1,), (1,)), ((), ())),
                                 preferred_element_type=F32)
            gcols = slice(g * gw, (g + 1) * gw)
            st = state_ref[:, gcols]
            y_off = jnp.dot(c_gb, st.astype(BF16), preferred_element_type=F32) * e_exp[:, gcols]
            state_ref[:, gcols] = e_exp[L - 1:L, gcols] * st + jnp.dot(
                b_gb.astype(F32).T.astype(BF16), xw[:, gcols], preferred_element_type=F32)
            for pr in range(hpg // 2):
                ms = []
                for e in range(2):
                    h = g * hpg + pr * 2 + e
                    seg = a_l2[:, AUX_DT0 + h:AUX_DT0 + h + 1] - a_l2t[AUX_DT0 + h:AUX_DT0 + h + 1, :]
                    lm = jnp.exp2(jnp.where(causal, seg, -jnp.inf))
                    ms.append((cb * lm).astype(BF16))
                lhs = jnp.concatenate(ms, axis=1)
                p0 = g * gw + pr * LANES
                xpair = xd[:, p0:p0 + LANES]
                zero = jnp.zeros_like(xpair)
                rhs = jnp.concatenate([jnp.where(first_half, xpair, zero),
                                       jnp.where(first_half, zero, xpair)], axis=0)
                y_parts.append(jnp.dot(lhs, rhs, preferred_element_type=F32)
                               + y_off[:, pr * LANES:(pr + 1) * LANES])
        y = jnp.concatenate(y_parts, axis=1) + dexp_ref[...] * x_c
        y = y * zs_ref[r, :].astype(F32)
        outs = []
        for g in range(SSD_GROUPS):
            yg = y[:, g * gw:(g + 1) * gw]
            yg = yg * lax.rsqrt(jnp.mean(yg * yg, axis=-1, keepdims=True) + NORM_EPS)
            outs.append(yg)
        y_ref[r, :] = (jnp.concatenate(outs, axis=1) * nw_ref[...]).astype(y_ref.dtype)


def _ssd(proj, aux, auxt, dtb128, alog128, dtbt, alogt, dexp, nw, xp, B, S, ts=512):
    ns = S // ts
    T = B * S
    return pl.pallas_call(
        _ssd_kernel,
        grid=(B, ns),
        in_specs=[
            pl.BlockSpec((ts, SSD_WIDTH), lambda b, i: (b * ns + i, COL_Z // SSD_WIDTH)),
            pl.BlockSpec((ts, SSD_WIDTH), lambda b, i: (b * ns + i, COL_X // SSD_WIDTH)),
            pl.BlockSpec((ts, BC_WIDTH), lambda b, i: (b * ns + i, COL_BC // BC_WIDTH)),
            pl.BlockSpec((ts, LANES), lambda b, i: (b * ns + i, 0)),
            pl.BlockSpec((AUXT_ROWS, ts), lambda b, i: (0, b * ns + i)),
            pl.BlockSpec((1, LANES), lambda b, i: (0, 0)),
            pl.BlockSpec((1, LANES), lambda b, i: (0, 0)),
            pl.BlockSpec((AUXT_ROWS, 1), lambda b, i: (0, 0)),
            pl.BlockSpec((AUXT_ROWS, 1), lambda b, i: (0, 0)),
            pl.BlockSpec((1, SSD_WIDTH), lambda b, i: (0, 0)),
            pl.BlockSpec((1, SSD_WIDTH), lambda b, i: (0, 0)),
            pl.BlockSpec((LANES, SSD_WIDTH), lambda b, i: (0, 0)),
        ],
        out_specs=pl.BlockSpec((ts, SSD_WIDTH), lambda b, i: (b * ns + i, 0)),
        out_shape=jax.ShapeDtypeStruct((T, SSD_WIDTH), BF16),
        scratch_shapes=[pltpu.VMEM((SSD_STATE, SSD_WIDTH), F32)],
        compiler_params=_cparams(("parallel", "arbitrary")),
        name="ssd",
    )(proj, proj, proj, aux, auxt, dtb128, alog128, dtbt, alogt, dexp, nw, xp)


def _ssd_expand_matrix():
    rows = jnp.arange(LANES)
    slot = rows % PACK_STRIDE
    part = rows // PACK_STRIDE
    head = slot - AUX_DT0
    cols = jnp.arange(SSD_WIDTH) // SSD_HEAD_DIM
    valid = (part < 3) & (head >= 0) & (head < SSD_HEADS)
    return jnp.where(valid[:, None] & (head[:, None] == cols[None, :]), 1.0, 0.0).astype(BF16)


POOL_HALO = 16


def _merge_kernel(x_ref, gates_ref, pool_ref, prev_ref, attn_ref, ssd_ref, mixw_ref, pscale_ref,
                  ppool_ref, pattn_ref, pssd_ref, wout_ref, o_ref):
    tm = x_ref.shape[0]
    i = pl.program_id(1)
    gdim = POOL_WIDTH // len(POOL_WINDOWS)

    prev = prev_ref[...].astype(F32)
    prev = jnp.where(i == 0, jnp.zeros_like(prev), prev)
    head_pos = i * tm + lax.broadcasted_iota(jnp.int32, (POOL_HALO, gdim), 0)
    ys = []
    for g, w in enumerate(POOL_WINDOWS):
        assert w <= POOL_HALO and w & (w - 1) == 0
        cols = slice(g * gdim, (g + 1) * gdim)
        v = pool_ref[:, cols].astype(F32)
        acc = jnp.concatenate([prev[:, cols], v], axis=0)
        span = 1
        while span < w:
            acc = acc + pltpu.roll(acc, span, 0)
            span *= 2
        inv_cnt = jnp.concatenate(
            [1.0 / jnp.minimum(head_pos + 1, w).astype(F32),
             jnp.full((tm - POOL_HALO, gdim), 1.0 / w, F32)], axis=0)
        d = (acc[POOL_HALO:, :] * inv_cnt - v).astype(BF16)
        ys.append(jnp.dot(d, mixw_ref[g], preferred_element_type=F32))
    ypre = (jnp.concatenate(ys, axis=1) * pscale_ref[...]).astype(BF16)
    y_pool = jnp.dot(ypre, ppool_ref[...], preferred_element_type=F32)
    y_attn = jnp.dot(attn_ref[...], pattn_ref[...], preferred_element_type=F32)
    y_ssd = jnp.dot(ssd_ref[...], pssd_ref[...], preferred_element_type=F32)
    def gate(k):
        x = gates_ref[:, k * D_MODEL:(k + 1) * D_MODEL].astype(F32)
        return 0.5 * jnp.tanh(0.5 * x) + 0.5

    merged = gate(0) * y_pool + gate(1) * y_attn + gate(2) * y_ssd
    o_ref[...] = x_ref[...] + jnp.dot(merged.astype(BF16), wout_ref[...],
                                      preferred_element_type=F32)


def _merge(x2d, proj, attn, yssd, mixw, pscale, ppool, pattn, pssd, wout, layer, B, S, tm=512):
    ns = S // tm
    T = B * S
    hb = tm // POOL_HALO
    const2 = lambda b, i: (0, 0)
    stacked = lambda *shape: pl.BlockSpec((None,) + shape, lambda b, i: (layer,) + (0,) * len(shape))
    return pl.pallas_call(
        _merge_kernel,
        grid=(B, ns),
        in_specs=[
            pl.BlockSpec((tm, D_MODEL), lambda b, i: (b * ns + i, 0)),
            pl.BlockSpec((tm, 3 * D_MODEL), lambda b, i: (b * ns + i, 0)),
            pl.BlockSpec((tm, POOL_WIDTH), lambda b, i: (b * ns + i, COL_POOL // POOL_WIDTH)),
            pl.BlockSpec((POOL_HALO, POOL_WIDTH),
                         lambda b, i: (jnp.maximum((b * ns + i) * hb - 1, 0), COL_POOL // POOL_WIDTH)),
            pl.BlockSpec((tm, ATTN_WIDTH), lambda b, i: (b * ns + i, 0)),
            pl.BlockSpec((tm, SSD_WIDTH), lambda b, i: (b * ns + i, 0)),
            stacked(len(POOL_WINDOWS), LANES, LANES),
            pl.BlockSpec((1, POOL_WIDTH), const2),
            stacked(POOL_WIDTH, D_MODEL),
            stacked(ATTN_WIDTH, D_MODEL),
            stacked(SSD_WIDTH, D_MODEL),
            stacked(D_MODEL, D_MODEL),
        ],
        out_specs=pl.BlockSpec((tm, D_MODEL), lambda b, i: (b * ns + i, 0)),
        out_shape=jax.ShapeDtypeStruct((T, D_MODEL), F32),
        compiler_params=_cparams(("parallel", "parallel")),
        name="merge",
    )(x2d, proj, proj, proj, attn, yssd, mixw, pscale, ppool, pattn, pssd, wout)


FFN_CHUNK = 256


def _ffn_kernel(x_ref, nw_ref, wup_ref, cw_ref, cb_ref, wd_ref, fnw_ref,
                o_ref, u_ref, act_ref, stage_ref, car_ref, *, final_norm):
    tm = x_ref.shape[0]
    halo = SUBLANES
    nch = FFN_DIM // FFN_CHUNK

    x = x_ref[...]
    ms = jnp.mean(x * x, axis=-1, keepdims=True)
    u_ref[...] = (x * lax.rsqrt(ms + NORM_EPS) * nw_ref[...]).astype(BF16)

    @pl.when(pl.program_id(1) == 0)
    def _():
        car_ref[...] = jnp.zeros_like(car_ref)

    def cols_of(c, half):
        start = half * FFN_DIM + c * FFN_CHUNK
        return slice(start, start + FFN_CHUNK)

    def up(c):
        for half in range(2):
            stage_ref[2 * (c % 2) + half] = jnp.dot(u_ref[...], wup_ref[:, cols_of(c, half)],
                                                    preferred_element_type=F32)

    def conv(c, half):
        cols = cols_of(c, half)
        h = stage_ref[2 * (c % 2) + half]
        ext = jnp.concatenate([car_ref[:, cols], h], axis=0)
        car_ref[:, cols] = h[tm - halo:tm, :]
        out = (cw_ref[2:3, cols] * ext + cw_ref[1:2, cols] * pltpu.roll(ext, 1, 0)
               + cw_ref[0:1, cols] * pltpu.roll(ext, 2, 0))
        return out[halo:, :] + cb_ref[:, cols]

    up(0)
    for c in range(nch):
        if c + 1 < nch:
            up(c + 1)
        act_ref[:, c * FFN_CHUNK:(c + 1) * FFN_CHUNK] = (_silu(conv(c, 0)) * conv(c, 1)).astype(BF16)

    y = x_ref[...] + jnp.dot(act_ref[...], wd_ref[...], preferred_element_type=F32)
    if final_norm:
        ms = jnp.mean(y * y, axis=-1, keepdims=True)
        y = y * lax.rsqrt(ms + NORM_EPS) * fnw_ref[...]
    o_ref[...] = y


def _ffn(x2d, nw, wup, cw, cb, wdown, fnw, layer, B, S, final_norm, tm=512):
    ns = S // tm
    T = B * S
    whole = lambda shape: pl.BlockSpec(shape, lambda b, i: (0, 0))
    stacked = lambda *shape: pl.BlockSpec((None,) + shape, lambda b, i: (layer, 0, 0))
    return pl.pallas_call(
        functools.partial(_ffn_kernel, final_norm=final_norm),
        grid=(B, ns),
        in_specs=[
            pl.BlockSpec((tm, D_MODEL), lambda b, i: (b * ns + i, 0)),
            whole((1, D_MODEL)),
            stacked(D_MODEL, 2 * FFN_DIM),
            whole((FFN_CONV, 2 * FFN_DIM)),
            whole((1, 2 * FFN_DIM)),
            stacked(FFN_DIM, D_MODEL),
            whole((1, D_MODEL)),
        ],
        out_specs=pl.BlockSpec((tm, D_MODEL), lambda b, i: (b * ns + i, 0)),
        out_shape=jax.ShapeDtypeStruct((T, D_MODEL), F32),
        scratch_shapes=[
            pltpu.VMEM((tm, D_MODEL), BF16),
            pltpu.VMEM((tm, FFN_DIM), BF16),
            pltpu.VMEM((4, tm, FFN_CHUNK), F32),
            pltpu.VMEM((SUBLANES, 2 * FFN_DIM), F32),
        ],
        compiler_params=_cparams(("parallel", "arbitrary")),
        name="ffn",
    )(x2d, nw, wup, cw, cb, wdown, fnw)


def _pad_lanes(v, offset, width=LANES):
    return jnp.zeros((1, width), F32).at[0, offset:offset + v.shape[0]].set(v.astype(F32))


IN_POOL, IN_QKV, IN_F, IN_Z, IN_X, IN_BC, IN_DT, IN_GATES = 0, 512, 2048, 2056, 3080, 4104, 4616, 4632
IN_TOTAL = IN_GATES + 3 * D_MODEL


WPREP_CHUNK = 512
WPREP_SRC = tuple(start + WPREP_CHUNK * k
                  for start, width in ((IN_GATES, 3 * D_MODEL), (IN_Z, SSD_WIDTH), (IN_BC, BC_WIDTH),
                                       (IN_POOL, POOL_WIDTH), (IN_X, SSD_WIDTH),
                                       (IN_QKV, 3 * ATTN_WIDTH))
                  for k in range(width // WPREP_CHUNK))
assert len(WPREP_SRC) * WPREP_CHUNK == PROJ_COLS and all(s % SUBLANES == 0 for s in WPREP_SRC)


def _wprep_kernel(src_ref, wt_ref, wp_ref):
    del src_ref
    wp_ref[...] = wt_ref[0].T.astype(BF16)


def _wprep(w_in):
    L, K, N = w_in.shape
    assert N == IN_TOTAL
    wt = jnp.swapaxes(w_in, 1, 2)
    w_perm = pl.pallas_call(
        _wprep_kernel,
        grid_spec=pltpu.PrefetchScalarGridSpec(
            num_scalar_prefetch=1,
            grid=(L, len(WPREP_SRC)),
            in_specs=[pl.BlockSpec((pl.Element(1), pl.Element(WPREP_CHUNK), pl.Element(K)),
                                   lambda l, c, src: (l, pl.multiple_of(src[c], SUBLANES), 0))],
            out_specs=pl.BlockSpec((None, K, WPREP_CHUNK), lambda l, c, src: (l, 0, c)),
        ),
        out_shape=jax.ShapeDtypeStruct((L, K, PROJ_COLS), BF16),
        compiler_params=_cparams(("parallel", "parallel")),
        name="wprep",
    )(jnp.asarray(WPREP_SRC, jnp.int32), wt)
    def small_rows(f_ref, dt_ref, o_ref):
        pad = jnp.zeros((AUXT_ROWS - ATTN_HEADS - SSD_HEADS, K), F32)
        o_ref[...] = jnp.concatenate([f_ref[0], dt_ref[0], pad], axis=0).astype(BF16)

    rows_at = lambda start, n: pl.BlockSpec((pl.Element(1), pl.Element(n), pl.Element(K)),
                                            lambda l: (l, start, 0))
    wauxt = pl.pallas_call(
        small_rows,
        grid=(L,),
        in_specs=[rows_at(IN_F, ATTN_HEADS), rows_at(IN_DT, SSD_HEADS)],
        out_specs=pl.BlockSpec((None, AUXT_ROWS, K), lambda l: (l, 0, 0)),
        out_shape=jax.ShapeDtypeStruct((L, AUXT_ROWS, K), BF16),
        compiler_params=_cparams(("parallel",)),
        name="wprep_small",
    )(wt, wt)
    return w_perm, wauxt


def kernel(x, norm_mix, w_in, pool_mix, pool_scale, f_bias, ssd_conv_w, ssd_conv_b, ssd_dt_bias,
           ssd_a_log, ssd_d, ssd_norm, p_pool, p_attn, p_ssd, w_out, norm_ffn, ffn_up, ffn_conv_w,
           ffn_conv_b, ffn_down, norm_final):
    B, S, D = x.shape
    depth = w_in.shape[0]
    T = B * S
    x2d = x.reshape(T, D)
    place = _fox_placement()
    xp = _ssd_expand_matrix()
    fnw = norm_final.reshape(1, D)
    w_perm, wauxt = _wprep(w_in)
    mixw, ppool, pattn, pssd, wout, wup, wdown = (
        w.astype(BF16) for w in (pool_mix, p_pool, p_attn, p_ssd, w_out, ffn_up, ffn_down))
    for l in range(depth):
        proj, aux, auxt = _in_proj(x2d, norm_mix[l].reshape(1, D), w_perm, wauxt, l,
                                   ssd_conv_w[l], ssd_conv_b[l].reshape(1, -1), S)
        qt, ka, vt = _fox_prep(proj, auxt, _pad_lanes(f_bias[l], AUX_F0, AUXT_ROWS).T, place, B, S,
                               ATTN_TILE)
        attn = _flash(qt, ka, vt, B, S, ATTN_TILE)
        dtb128 = _pad_lanes(ssd_dt_bias[l], AUX_DT0)
        alog128 = _pad_lanes(ssd_a_log[l], AUX_DT0)
        yssd = _ssd(proj, aux, auxt, dtb128, alog128,
                    _pad_lanes(ssd_dt_bias[l], AUX_DT0, AUXT_ROWS).T,
                    _pad_lanes(ssd_a_log[l], AUX_DT0, AUXT_ROWS).T,
                    jnp.repeat(ssd_d[l], SSD_HEAD_DIM).reshape(1, -1), ssd_norm[l].reshape(1, -1),
                    xp, B, S)
        x2d = _merge(x2d, proj, attn, yssd, mixw, pool_scale[l].reshape(1, -1),
                     ppool, pattn, pssd, wout, l, B, S)
        x2d = _ffn(x2d, norm_ffn[l].reshape(1, D), wup, ffn_conv_w[l],
                   ffn_conv_b[l].reshape(1, -1), wdown, fnw, l, B, S,
                   final_norm=(l == depth - 1))
    return x2d.reshape(B, S, D)
```

```python
import functools

import jax
import jax.numpy as jnp
from jax import lax
from jax.experimental import pallas as pl
from jax.experimental.pallas import tpu as pltpu

F32 = jnp.float32
BF16 = jnp.bfloat16

D_MODEL = 1024
NORM_EPS = 1e-6
POOL_WINDOWS = (2, 4, 8, 16)
POOL_WIDTH = 512
ATTN_HEADS = 8
ATTN_HEAD_DIM = 64
ATTN_WIDTH = 512
SSD_HEADS = 16
SSD_HEAD_DIM = 64
SSD_WIDTH = 1024
SSD_GROUPS = 2
SSD_STATE = 128
SSD_CONV = 4
SSD_CHUNK = 128
SSD_CONV_CH = 1536
FFN_DIM = 2816
FFN_CONV = 3

LANES = 128
SUBLANES = 8

COL_GATES = 0
COL_Z = 3072
COL_BC = 4096
COL_POOL = 4608
COL_XLO = 5120
COL_Q = 5632
COL_XHI = 6144
COL_K = 6656
COL_V = 7168
PROJ_COLS = 7680
BC_WIDTH = 2 * SSD_GROUPS * SSD_STATE
X_HALF = SSD_WIDTH // 2
PROJ_TILE = 1536
PROJ_CHUNK = 256
TILE_ZBC = COL_Z // PROJ_TILE
TILE_POOLXQ = COL_POOL // PROJ_TILE
TILE_XKV = COL_XHI // PROJ_TILE
assert COL_Z == TILE_ZBC * PROJ_TILE and COL_BC == COL_Z + SSD_WIDTH
assert COL_POOL == TILE_POOLXQ * PROJ_TILE and COL_XLO == COL_POOL + POOL_WIDTH
assert COL_XHI == TILE_XKV * PROJ_TILE and COL_Q == COL_XLO + X_HALF
AUX_F0 = 0
AUX_DT0 = 8
AUXT_ROWS = 32

VMEM_LIMIT = 56 * 1024 * 1024


def _cparams(sem):
    return pltpu.CompilerParams(dimension_semantics=sem, vmem_limit_bytes=VMEM_LIMIT)


def _split3(x):
    hi = x.astype(BF16).astype(F32)
    r1 = x - hi
    mid = r1.astype(BF16).astype(F32)
    lo = (r1 - mid).astype(BF16).astype(F32)
    return hi, mid, lo


def _pack3(x, lane, stride):
    hi, mid, lo = _split3(x)
    packed = jnp.where(lane < stride, hi,
                       jnp.where(lane < 2 * stride, pltpu.roll(mid, stride, 1),
                                 pltpu.roll(lo, 2 * stride, 1)))
    return packed.astype(BF16)


def _silu(x):
    h = 0.5 * x
    return h + h * jnp.tanh(h)


def _inproj_kernel(x_ref, nw_ref, w_ref, wauxt_ref, cw_ref, cb_ref,
                   out_ref, aux_ref, auxt_ref, u_ref, halo_ref, acc_ref, *, tiles_per_seq):
    i = pl.program_id(0)
    j = pl.program_id(1)
    tm = x_ref.shape[0]
    halo = SUBLANES

    def conv_silu(acc, wcols):
        prev = halo_ref[:, wcols]
        prev = jnp.where(i % tiles_per_seq == 0, jnp.zeros_like(prev), prev)
        ext = jnp.concatenate([prev, acc], axis=0)
        halo_ref[:, wcols] = acc[tm - halo:tm, :]
        ext1 = pltpu.roll(ext, 1, 0)
        near = cw_ref[3:4, wcols] * ext + cw_ref[2:3, wcols] * ext1
        far = cw_ref[1:2, wcols] * ext + cw_ref[0:1, wcols] * ext1
        conv = (near + pltpu.roll(far, 2, 0))[halo:, :] + cb_ref[:, wcols]
        return _silu(conv)

    def tile(epilogues, lhs=None, order=None):
        order = list(range(len(epilogues))) if order is None else order
        assert sorted(order) == list(range(len(epilogues)))

        def chunk_dot(n):
            c = order[n]
            acc_ref[n % 2] = jnp.dot(u_ref[...] if lhs is None else lhs,
                                     w_ref[:, c * PROJ_CHUNK:(c + 1) * PROJ_CHUNK],
                                     preferred_element_type=F32)

        chunk_dot(0)
        for n, c in enumerate(order):
            if n + 1 < len(order):
                chunk_dot(n + 1)
            out_ref[:, c * PROJ_CHUNK:(c + 1) * PROJ_CHUNK] = epilogues[c](
                acc_ref[n % 2]).astype(out_ref.dtype)

    def conv_at(off):
        return lambda acc: conv_silu(acc, slice(off, off + PROJ_CHUNK))

    plain = lambda acc: acc
    nchunk = out_ref.shape[1] // PROJ_CHUNK

    @pl.when(j == 0)
    def _():
        x = x_ref[...]
        ms = jnp.mean(x * x, axis=-1, keepdims=True)
        u = (x * lax.rsqrt(ms + NORM_EPS) * nw_ref[...]).astype(BF16)
        u_ref[...] = u
        auxt = lax.dot_general(wauxt_ref[...], u, (((1,), (1,)), ((), ())),
                               preferred_element_type=F32)
        auxt_ref[...] = auxt
        aux_ref[...] = jnp.concatenate(
            [auxt, jnp.zeros((LANES - AUXT_ROWS, tm), F32)], axis=0).T
        tile([plain] * nchunk, lhs=u)

    @pl.when((j != 0) & (j != TILE_ZBC) & (j != TILE_POOLXQ) & (j != TILE_XKV))
    def _():
        tile([plain] * nchunk)

    def convs(first, width):
        return [conv_at(first + k * PROJ_CHUNK) for k in range(width // PROJ_CHUNK)]

    def plains(width):
        return [plain] * (width // PROJ_CHUNK)

    assert nchunk == 6 and BC_WIDTH == X_HALF == 2 * PROJ_CHUNK

    @pl.when(j == TILE_ZBC)
    def _():
        tile([_silu] * (SSD_WIDTH // PROJ_CHUNK) + convs(SSD_WIDTH, BC_WIDTH),
             order=[4, 0, 1, 5, 2, 3])

    @pl.when(j == TILE_POOLXQ)
    def _():
        tile(plains(POOL_WIDTH) + convs(0, X_HALF) + plains(ATTN_WIDTH), order=[2, 0, 1, 3, 4, 5])

    @pl.when(j == TILE_XKV)
    def _():
        tile(convs(X_HALF, X_HALF) + plains(2 * ATTN_WIDTH), order=[0, 2, 3, 1, 4, 5])


def _in_proj(x2d, norm_w, w_perm, wauxt, layer, cw, cb, S, tm=1024, tn=PROJ_TILE):
    T = x2d.shape[0]
    return pl.pallas_call(
        functools.partial(_inproj_kernel, tiles_per_seq=S // tm),
        grid=(T // tm, PROJ_COLS // tn),
        in_specs=[
            pl.BlockSpec((tm, D_MODEL), lambda i, j: (i, 0)),
            pl.BlockSpec((1, D_MODEL), lambda i, j: (0, 0)),
            pl.BlockSpec((None, D_MODEL, tn), lambda i, j: (layer, 0, j)),
            pl.BlockSpec((None, AUXT_ROWS, D_MODEL), lambda i, j: (layer, 0, 0)),
            pl.BlockSpec((SSD_CONV, SSD_CONV_CH), lambda i, j: (0, 0)),
            pl.BlockSpec((1, SSD_CONV_CH), lambda i, j: (0, 0)),
        ],
        out_specs=[
            pl.BlockSpec((tm, tn), lambda i, j: (i, j)),
            pl.BlockSpec((tm, LANES), lambda i, j: (i, 0)),
            pl.BlockSpec((AUXT_ROWS, tm), lambda i, j: (0, i)),
        ],
        out_shape=[
            jax.ShapeDtypeStruct((T, PROJ_COLS), BF16),
            jax.ShapeDtypeStruct((T, LANES), F32),
            jax.ShapeDtypeStruct((AUXT_ROWS, T), F32),
        ],
        scratch_shapes=[pltpu.VMEM((tm, D_MODEL), BF16),
                        pltpu.VMEM((SUBLANES, SSD_CONV_CH), F32),
                        pltpu.VMEM((2, tm, PROJ_CHUNK), F32)],
        compiler_params=_cparams(("arbitrary", "arbitrary")),
        name="in_proj",
    )(x2d, norm_w, w_perm, wauxt, cw, cb)


AUG = 2 * ATTN_HEAD_DIM
ATTN_TILE = 512
VT_ROWS = ATTN_HEAD_DIM + 16


def _foxprep_kernel(q_ref, k_ref, v_ref, auxt_ref, fbt_ref, triu_ref, place_ref,
                    qt_ref, ka_ref, vt_ref, carry_ref):
    ts = q_ref.shape[0]
    nh = ATTN_HEADS
    R = AUXT_ROWS

    @pl.when(pl.program_id(1) == 0)
    def _():
        carry_ref[...] = jnp.zeros_like(carry_ref)

    logf = jax.nn.log_sigmoid(auxt_ref[...] + fbt_ref[...])
    parts = jnp.concatenate(_split3(logf), axis=0).astype(BF16)
    cs = jnp.dot(parts, triu_ref[...], preferred_element_type=F32)
    ct = cs[0:R] + cs[R:2 * R] + cs[2 * R:3 * R] + carry_ref[...]
    carry_ref[...] = ct[:, ts - 1:ts]

    ch, cm, cl = _split3(ct)
    stacked = jnp.concatenate([ch[0:nh], cm[0:nh], cl[0:nh],
                               jnp.zeros((LANES - 3 * nh, ts), F32)], axis=0)
    aug = jnp.dot(stacked.T.astype(BF16), place_ref[...], preferred_element_type=F32)
    lane = lax.broadcasted_iota(jnp.int32, (ts, LANES), 1)
    is_q = lane < ATTN_HEAD_DIM
    is_pos = lane < ATTN_HEAD_DIM + 3
    ones_hi = jnp.where((lane >= ATTN_HEAD_DIM + 3) & (lane < ATTN_HEAD_DIM + 6), 1.0, 0.0)
    ones_lo = jnp.where((lane >= ATTN_HEAD_DIM) & is_pos, 1.0, 0.0)
    scale = ATTN_HEAD_DIM ** -0.5
    row_id = lax.broadcasted_iota(jnp.int32, (VT_ROWS - ATTN_HEAD_DIM, ts), 0)
    ones_rows = jnp.where(row_id == 0, 1.0, 0.0)
    for h in range(ATTN_HEADS):
        pair = slice((h // 2) * LANES, (h // 2 + 1) * LANES)
        blk = slice(h * LANES, (h + 1) * LANES)
        qp = q_ref[:, pair].astype(F32)
        kp = k_ref[:, pair].astype(F32)
        if h % 2 == 1:
            qp = pltpu.roll(qp, ATTN_HEAD_DIM, 1)
            kp = pltpu.roll(kp, ATTN_HEAD_DIM, 1)
        qa = jnp.where(is_q, qp * scale, jnp.where(is_pos, aug[:, blk], ones_hi))
        qt_ref[0, h] = qa.T.astype(BF16)
        ka_ref[0, h] = jnp.where(is_q, kp, jnp.where(is_pos, ones_lo, aug[:, blk])).astype(BF16)
        vt = v_ref[:, pair].astype(F32).T
        vh = vt[(h % 2) * ATTN_HEAD_DIM:(h % 2 + 1) * ATTN_HEAD_DIM, :]
        vt_ref[0, h, 0] = jnp.concatenate([vh, ones_rows], axis=0).astype(BF16)


def _fox_prep(proj, auxt, fbt, place, B, S, ts):
    ns = S // ts
    triu = (jnp.arange(ts)[:, None] <= jnp.arange(ts)[None, :]).astype(BF16)
    return pl.pallas_call(
        _foxprep_kernel,
        grid=(B, ns),
        in_specs=[
            pl.BlockSpec((ts, ATTN_WIDTH), lambda b, i: (b * ns + i, COL_Q // ATTN_WIDTH)),
            pl.BlockSpec((ts, ATTN_WIDTH), lambda b, i: (b * ns + i, COL_K // ATTN_WIDTH)),
            pl.BlockSpec((ts, ATTN_WIDTH), lambda b, i: (b * ns + i, COL_V // ATTN_WIDTH)),
            pl.BlockSpec((AUXT_ROWS, ts), lambda b, i: (0, b * ns + i)),
            pl.BlockSpec((AUXT_ROWS, 1), lambda b, i: (0, 0)),
            pl.BlockSpec((ts, ts), lambda b, i: (0, 0)),
            pl.BlockSpec((LANES, ATTN_HEADS * AUG), lambda b, i: (0, 0)),
        ],
        out_specs=[
            pl.BlockSpec((1, ATTN_HEADS, AUG, ts), lambda b, i: (b, 0, 0, i)),
            pl.BlockSpec((1, ATTN_HEADS, ts, AUG), lambda b, i: (b, 0, i, 0)),
            pl.BlockSpec((1, ATTN_HEADS, 1, VT_ROWS, ts), lambda b, i: (b, 0, i, 0, 0)),
        ],
        out_shape=[
            jax.ShapeDtypeStruct((B, ATTN_HEADS, AUG, S), BF16),
            jax.ShapeDtypeStruct((B, ATTN_HEADS, S, AUG), BF16),
            jax.ShapeDtypeStruct((B, ATTN_HEADS, ns, VT_ROWS, ts), BF16),
        ],
        scratch_shapes=[pltpu.VMEM((AUXT_ROWS, 1), F32)],
        compiler_params=_cparams(("parallel", "arbitrary")),
        name="fox_prep",
    )(proj, proj, proj, auxt, fbt, triu, place)


def _fox_placement():
    rows = jnp.arange(LANES)
    part = rows // ATTN_HEADS
    head = rows % ATTN_HEADS
    cols = jnp.arange(ATTN_HEADS * AUG)
    chead = cols // AUG
    clane = cols % AUG
    valid = (part[:, None] < 3) & (head[:, None] == chead[None, :])
    plus = valid & (clane[None, :] == ATTN_HEAD_DIM + part[:, None])
    minus = valid & (clane[None, :] == ATTN_HEAD_DIM + 3 + part[:, None])
    return (plus.astype(F32) - minus.astype(F32)).astype(BF16)


def _flash_kernel(qt_ref, k_ref, vt_ref, o_ref, m_ref, acc_ref, sa_ref, sb_ref):
    i = pl.program_id(2)
    t = qt_ref.shape[3]
    m_ref[...] = jnp.full_like(m_ref, -jnp.inf)
    acc_ref[...] = jnp.zeros_like(acc_ref)

    def scores(dst_ref, j):
        for h in range(2):
            k = k_ref[0, h, pl.ds(pl.multiple_of(j * t, t), t), :]
            dst_ref[h] = jnp.dot(k, qt_ref[0, h], preferred_element_type=F32)

    def softmax_pv(src_ref, j, masked):
        for h in range(2):
            s = src_ref[h]
            if masked:
                row = lax.broadcasted_iota(jnp.int32, (t, t), 0)
                col = lax.broadcasted_iota(jnp.int32, (t, t), 1)
                s = jnp.where(row <= col, s, -jnp.inf)
            m_prev = m_ref[h]
            m_next = jnp.maximum(m_prev, jnp.max(s, axis=0, keepdims=True))
            alpha = jnp.exp(m_prev - m_next)
            p = jnp.exp(s - m_next).astype(BF16)
            acc_ref[h] = alpha * acc_ref[h] + jnp.dot(vt_ref[0, h, j], p,
                                                      preferred_element_type=F32)
            m_ref[h] = m_next

    last = jnp.maximum(i - 1, 0)
    scores(sa_ref, i)
    scores(sb_ref, 0)
    softmax_pv(sa_ref, i, True)

    def body(jj, carry):
        j0 = 2 * jj
        scores(sa_ref, j0 + 1)
        softmax_pv(sb_ref, j0, False)
        scores(sb_ref, jnp.minimum(j0 + 2, last))
        softmax_pv(sa_ref, j0 + 1, False)
        return carry

    lax.fori_loop(0, i // 2, body, 0)

    @pl.when(i % 2 == 1)
    def _():
        softmax_pv(sb_ref, i - 1, False)

    outs = []
    for h in range(2):
        acc = acc_ref[h]
        outs.append(acc[0:ATTN_HEAD_DIM, :] / acc[ATTN_HEAD_DIM:ATTN_HEAD_DIM + 1, :])
    o_ref[...] = jnp.concatenate(outs, axis=0).T.astype(o_ref.dtype)


def _flash(qt, ka, vt, B, S, t):
    n = S // t
    T = B * S
    hp = ATTN_HEADS // 2
    return pl.pallas_call(
        _flash_kernel,
        grid=(B, hp, n),
        in_specs=[
            pl.BlockSpec((1, 2, AUG, t), lambda b, h, i: (b, h, 0, i)),
            pl.BlockSpec((1, 2, S, AUG), lambda b, h, i: (b, h, 0, 0)),
            pl.BlockSpec((1, 2, n, VT_ROWS, t), lambda b, h, i: (b, h, 0, 0, 0)),
        ],
        out_specs=pl.BlockSpec((t, LANES), lambda b, h, i: (b * n + i, h)),
        out_shape=jax.ShapeDtypeStruct((T, ATTN_WIDTH), BF16),
        scratch_shapes=[
            pltpu.VMEM((2, 1, t), F32),
            pltpu.VMEM((2, VT_ROWS, t), F32),
            pltpu.VMEM((2, t, t), F32),
            pltpu.VMEM((2, t, t), F32),
        ],
        compiler_params=_cparams(("parallel", "parallel", "arbitrary")),
        name="flash",
    )(qt, ka, vt)


PACK_STRIDE = 32
LOG2E = 1.4426950408889634


def _ssd_kernel(zs_ref, xlo_ref, xhi_ref, bc_ref, aux_ref, auxt_ref, dtb_ref, alog_ref,
                dtbt_ref, alogt_ref, dexp_ref, nw_ref, xp_ref,
                y_ref, state_ref):
    ts = zs_ref.shape[0]
    L = SSD_CHUNK

    @pl.when(pl.program_id(1) == 0)
    def _():
        state_ref[...] = jnp.zeros_like(state_ref)

    dt =jax.nn.softplus(aux_ref[...] + dtb_ref[...])
    a = dt * (-jnp.exp(alog_ref[...]))
    dtt = jax.nn.softplus(auxt_ref[...] + dtbt_ref[...])
    at = dtt * (-jnp.exp(alogt_ref[...]))

    row = lax.broadcasted_iota(jnp.int32, (L, L), 0)
    col = lax.broadcasted_iota(jnp.int32, (L, L), 1)
    causal = col <= row
    tril = causal.astype(BF16)
    triu = (row <= col).astype(BF16)
    lane = lax.broadcasted_iota(jnp.int32, (L, LANES), 1)
    first_half = lane < SSD_HEAD_DIM
    xp = xp_ref[...]
    gw = SSD_WIDTH // SSD_GROUPS
    hpg = SSD_HEADS // SSD_GROUPS

    for c in range(ts // L):
        r = slice(c * L, (c + 1) * L)
        a_c = a[r]
        ah, am, al = _split3(a_c)
        a3 = jnp.concatenate([ah, am, al], axis=1).astype(BF16)
        cs3 = jnp.dot(tril, a3, preferred_element_type=F32)
        a_cs = cs3[:, :LANES] + cs3[:, LANES:2 * LANES] + cs3[:, 2 * LANES:]
        th, tm_, tl = _split3(at[:, r])
        t3 = jnp.concatenate([th, tm_, tl], axis=0).astype(BF16)
        cst = jnp.dot(t3, triu, preferred_element_type=F32)
        a_cst = cst[0:AUXT_ROWS] + cst[AUXT_ROWS:2 * AUXT_ROWS] + cst[2 * AUXT_ROWS:]

        a_l2 = a_cs * LOG2E
        a_l2t = a_cst * LOG2E
        e_small = jnp.exp(a_cs)
        w_small = jnp.exp(a_cs[L - 1:L, :] - a_cs) * dt[r]
        expand = lambda v: jnp.dot(_pack3(v, lane, PACK_STRIDE), xp, preferred_element_type=F32)
        dt_exp = expand(dt[r])
        w_exp = expand(w_small)
        e_exp = expand(e_small)

        x_c = jnp.concatenate([xlo_ref[r, :], xhi_ref[r, :]], axis=1).astype(F32)
        xd = (x_c * dt_exp).astype(BF16)
        xw = (x_c * w_exp).astype(BF16)
        y_parts = []
        for g in range(SSD_GROUPS):
            b_gb = bc_ref[r, g * SSD_STATE:(g + 1) * SSD_STATE]
            c_gb = bc_ref[r, (SSD_GROUPS + g) * SSD_STATE:(SSD_GROUPS + g + 1) * SSD_STATE]
            cb = lax.dot_general(c_gb, b_gb, (((1,), (1,)), ((), ())),
                                 preferred_element_type=F32)
            gcols = slice(g * gw, (g + 1) * gw)
            st = state_ref[:, gcols]
            y_off = jnp.dot(c_gb, st.astype(BF16), preferred_element_type=F32) * e_exp[:, gcols]
            state_ref[:, gcols] = e_exp[L - 1:L, gcols] * st + jnp.dot(
                b_gb.astype(F32).T.astype(BF16), xw[:, gcols], preferred_element_type=F32)
            for pr in range(hpg // 2):
                ms = []
                for e in range(2):
                    h = g * hpg + pr * 2 + e
                    seg = a_l2[:, AUX_DT0 + h:AUX_DT0 + h + 1] - a_l2t[AUX_DT0 + h:AUX_DT0 + h + 1, :]
                    lm = jnp.exp2(jnp.where(causal, seg, -jnp.inf))
                    ms.append((cb * lm).astype(BF16))
                lhs = jnp.concatenate(ms, axis=1)
                p0 = g * gw + pr * LANES
                xpair = xd[:, p0:p0 + LANES]
                zero = jnp.zeros_like(xpair)
                rhs = jnp.concatenate([jnp.where(first_half, xpair, zero),
                                       jnp.where(first_half, zero, xpair)], axis=0)
                y_parts.append(jnp.dot(lhs, rhs, preferred_element_type=F32)
                               + y_off[:, pr * LANES:(pr + 1) * LANES])
        y = jnp.concatenate(y_parts, axis=1) + dexp_ref[...] * x_c
        y = y * zs_ref[r, :].astype(F32)
        outs = []
        for g in range(SSD_GROUPS):
            yg = y[:, g * gw:(g + 1) * gw]
            yg = yg * lax.rsqrt(jnp.mean(yg * yg, axis=-1, keepdims=True) + NORM_EPS)
            outs.append(yg)
        y_ref[r, :] = (jnp.concatenate(outs, axis=1) * nw_ref[...]).astype(y_ref.dtype)


def _ssd(proj, aux, auxt, dtb128, alog128, dtbt, alogt, dexp, nw, xp, B, S, ts=512):
    ns = S // ts
    T = B * S
    return pl.pallas_call(
        _ssd_kernel,
        grid=(B, ns),
        in_specs=[
            pl.BlockSpec((ts, SSD_WIDTH), lambda b, i: (b * ns + i, COL_Z // SSD_WIDTH)),
            pl.BlockSpec((ts, X_HALF), lambda b, i: (b * ns + i, COL_XLO // X_HALF)),
            pl.BlockSpec((ts, X_HALF), lambda b, i: (b * ns + i, COL_XHI // X_HALF)),
            pl.BlockSpec((ts, BC_WIDTH), lambda b, i: (b * ns + i, COL_BC // BC_WIDTH)),
            pl.BlockSpec((ts, LANES), lambda b, i: (b * ns + i, 0)),
            pl.BlockSpec((AUXT_ROWS, ts), lambda b, i: (0, b * ns + i)),
            pl.BlockSpec((1, LANES), lambda b, i: (0, 0)),
            pl.BlockSpec((1, LANES), lambda b, i: (0, 0)),
            pl.BlockSpec((AUXT_ROWS, 1), lambda b, i: (0, 0)),
            pl.BlockSpec((AUXT_ROWS, 1), lambda b, i: (0, 0)),
            pl.BlockSpec((1, SSD_WIDTH), lambda b, i: (0, 0)),
            pl.BlockSpec((1, SSD_WIDTH), lambda b, i: (0, 0)),
            pl.BlockSpec((LANES, SSD_WIDTH), lambda b, i: (0, 0)),
        ],
        out_specs=pl.BlockSpec((ts, SSD_WIDTH), lambda b, i: (b * ns + i, 0)),
        out_shape=jax.ShapeDtypeStruct((T, SSD_WIDTH), BF16),
        scratch_shapes=[pltpu.VMEM((SSD_STATE, SSD_WIDTH), F32)],
        compiler_params=_cparams(("parallel", "arbitrary")),
        name="ssd",
    )(proj, proj, proj, proj, aux, auxt, dtb128, alog128, dtbt, alogt, dexp, nw, xp)


def _ssd_expand_matrix():
    rows = jnp.arange(LANES)
    slot = rows % PACK_STRIDE
    part = rows // PACK_STRIDE
    head = slot - AUX_DT0
    cols = jnp.arange(SSD_WIDTH) // SSD_HEAD_DIM
    valid = (part < 3) & (head >= 0) & (head < SSD_HEADS)
    return jnp.where(valid[:, None] & (head[:, None] == cols[None, :]), 1.0, 0.0).astype(BF16)


POOL_HALO = 16


def _merge_kernel(x_ref, gates_ref, pool_ref, prev_ref, attn_ref, ssd_ref, mixw_ref, pscale_ref,
                  ppool_ref, pattn_ref, pssd_ref, wout_ref, o_ref):
    tm = x_ref.shape[0]
    i = pl.program_id(1)
    gdim = POOL_WIDTH // len(POOL_WINDOWS)

    prev = prev_ref[...].astype(F32)
    prev = jnp.where(i == 0, jnp.zeros_like(prev), prev)
    head_pos = i * tm + lax.broadcasted_iota(jnp.int32, (POOL_HALO, gdim), 0)
    ys = []
    for g, w in enumerate(POOL_WINDOWS):
        assert w <= POOL_HALO and w & (w - 1) == 0
        cols = slice(g * gdim, (g + 1) * gdim)
        v = pool_ref[:, cols].astype(F32)
        acc = jnp.concatenate([prev[:, cols], v], axis=0)
        span = 1
        while span < w:
            acc = acc + pltpu.roll(acc, span, 0)
            span *= 2
        inv_cnt = jnp.concatenate(
            [1.0 / jnp.minimum(head_pos + 1, w).astype(F32),
             jnp.full((tm - POOL_HALO, gdim), 1.0 / w, F32)], axis=0)
        d = (acc[POOL_HALO:, :] * inv_cnt - v).astype(BF16)
        ys.append(jnp.dot(d, mixw_ref[g], preferred_element_type=F32))
    ypre = (jnp.concatenate(ys, axis=1) * pscale_ref[...]).astype(BF16)
    y_pool = jnp.dot(ypre, ppool_ref[...], preferred_element_type=F32)
    y_attn = jnp.dot(attn_ref[...], pattn_ref[...], preferred_element_type=F32)
    y_ssd = jnp.dot(ssd_ref[...], pssd_ref[...], preferred_element_type=F32)
    def gate(k):
        x = gates_ref[:, k * D_MODEL:(k + 1) * D_MODEL].astype(F32)
        return 0.5 * jnp.tanh(0.5 * x) + 0.5

    merged = gate(0) * y_pool + gate(1) * y_attn + gate(2) * y_ssd
    o_ref[...] = x_ref[...] + jnp.dot(merged.astype(BF16), wout_ref[...],
                                      preferred_element_type=F32)


def _merge(x2d, proj, attn, yssd, mixw, pscale, ppool, pattn, pssd, wout, layer, B, S, tm=512):
    ns = S // tm
    T = B * S
    hb = tm // POOL_HALO
    const2 = lambda b, i: (0, 0)
    stacked = lambda *shape: pl.BlockSpec((None,) + shape, lambda b, i: (layer,) + (0,) * len(shape))
    return pl.pallas_call(
        _merge_kernel,
        grid=(B, ns),
        in_specs=[
            pl.BlockSpec((tm, D_MODEL), lambda b, i: (b * ns + i, 0)),
            pl.BlockSpec((tm, 3 * D_MODEL), lambda b, i: (b * ns + i, 0)),
            pl.BlockSpec((tm, POOL_WIDTH), lambda b, i: (b * ns + i, COL_POOL // POOL_WIDTH)),
            pl.BlockSpec((POOL_HALO, POOL_WIDTH),
                         lambda b, i: (jnp.maximum((b * ns + i) * hb - 1, 0), COL_POOL // POOL_WIDTH)),
            pl.BlockSpec((tm, ATTN_WIDTH), lambda b, i: (b * ns + i, 0)),
            pl.BlockSpec((tm, SSD_WIDTH), lambda b, i: (b * ns + i, 0)),
            stacked(len(POOL_WINDOWS), LANES, LANES),
            pl.BlockSpec((1, POOL_WIDTH), const2),
            stacked(POOL_WIDTH, D_MODEL),
            stacked(ATTN_WIDTH, D_MODEL),
            stacked(SSD_WIDTH, D_MODEL),
            stacked(D_MODEL, D_MODEL),
        ],
        out_specs=pl.BlockSpec((tm, D_MODEL), lambda b, i: (b * ns + i, 0)),
        out_shape=jax.ShapeDtypeStruct((T, D_MODEL), F32),
        compiler_params=_cparams(("parallel", "parallel")),
        name="merge",
    )(x2d, proj, proj, proj, attn, yssd, mixw, pscale, ppool, pattn, pssd, wout)


FFN_CHUNK = 256


def _ffn_kernel(x_ref, nw_ref, wup_ref, cw_ref, cb_ref, wd_ref, fnw_ref,
                o_ref, u_ref, act_ref, stage_ref, car_ref, *, final_norm):
    tm = x_ref.shape[0]
    halo = SUBLANES
    nch = FFN_DIM // FFN_CHUNK

    x = x_ref[...]
    ms = jnp.mean(x * x, axis=-1, keepdims=True)
    u_ref[...] = (x * lax.rsqrt(ms + NORM_EPS) * nw_ref[...]).astype(BF16)

    @pl.when(pl.program_id(1) == 0)
    def _():
        car_ref[...] = jnp.zeros_like(car_ref)

    def cols_of(c, half):
        start = half * FFN_DIM + c * FFN_CHUNK
        return slice(start, start + FFN_CHUNK)

    def up(c):
        for half in range(2):
            stage_ref[2 * (c % 2) + half] = jnp.dot(u_ref[...], wup_ref[:, cols_of(c, half)],
                                                    preferred_element_type=F32)

    def conv(c, half):
        cols = cols_of(c, half)
        h = stage_ref[2 * (c % 2) + half]
        ext = jnp.concatenate([car_ref[:, cols], h], axis=0)
        car_ref[:, cols] = h[tm - halo:tm, :]
        out = (cw_ref[2:3, cols] * ext + cw_ref[1:2, cols] * pltpu.roll(ext, 1, 0)
               + cw_ref[0:1, cols] * pltpu.roll(ext, 2, 0))
        return out[halo:, :] + cb_ref[:, cols]

    up(0)
    for c in range(nch):
        if c + 1 < nch:
            up(c + 1)
        act_ref[:, c * FFN_CHUNK:(c + 1) * FFN_CHUNK] = (_silu(conv(c, 0)) * conv(c, 1)).astype(BF16)

    y = x_ref[...] + jnp.dot(act_ref[...], wd_ref[...], preferred_element_type=F32)
    if final_norm:
        ms = jnp.mean(y * y, axis=-1, keepdims=True)
        y = y * lax.rsqrt(ms + NORM_EPS) * fnw_ref[...]
    o_ref[...] = y


def _ffn(x2d, nw, wup, cw, cb, wdown, fnw, layer, B, S, final_norm, tm=512):
    ns = S // tm
    T = B * S
    whole = lambda shape: pl.BlockSpec(shape, lambda b, i: (0, 0))
    stacked = lambda *shape: pl.BlockSpec((None,) + shape, lambda b, i: (layer, 0, 0))
    return pl.pallas_call(
        functools.partial(_ffn_kernel, final_norm=final_norm),
        grid=(B, ns),
        in_specs=[
            pl.BlockSpec((tm, D_MODEL), lambda b, i: (b * ns + i, 0)),
            whole((1, D_MODEL)),
            stacked(D_MODEL, 2 * FFN_DIM),
            whole((FFN_CONV, 2 * FFN_DIM)),
            whole((1, 2 * FFN_DIM)),
            stacked(FFN_DIM, D_MODEL),
            whole((1, D_MODEL)),
        ],
        out_specs=pl.BlockSpec((tm, D_MODEL), lambda b, i: (b * ns + i, 0)),
        out_shape=jax.ShapeDtypeStruct((T, D_MODEL), F32),
        scratch_shapes=[
            pltpu.VMEM((tm, D_MODEL), BF16),
            pltpu.VMEM((tm, FFN_DIM), BF16),
            pltpu.VMEM((4, tm, FFN_CHUNK), F32),
            pltpu.VMEM((SUBLANES, 2 * FFN_DIM), F32),
        ],
        compiler_params=_cparams(("parallel", "arbitrary")),
        name="ffn",
    )(x2d, nw, wup, cw, cb, wdown, fnw)


def _pad_lanes(v, offset, width=LANES):
    return jnp.zeros((1, width), F32).at[0, offset:offset + v.shape[0]].set(v.astype(F32))


IN_POOL, IN_QKV, IN_F, IN_Z, IN_X, IN_BC, IN_DT, IN_GATES = 0, 512, 2048, 2056, 3080, 4104, 4616, 4632
IN_TOTAL = IN_GATES + 3 * D_MODEL


WPREP_CHUNK = 512
WPREP_SRC = tuple(start + WPREP_CHUNK * k
                  for start, width in ((IN_GATES, 3 * D_MODEL), (IN_Z, SSD_WIDTH), (IN_BC, BC_WIDTH),
                                       (IN_POOL, POOL_WIDTH), (IN_X, X_HALF),
                                       (IN_QKV, ATTN_WIDTH), (IN_X + X_HALF, X_HALF),
                                       (IN_QKV + ATTN_WIDTH, 2 * ATTN_WIDTH))
                  for k in range(width // WPREP_CHUNK))
assert len(WPREP_SRC) * WPREP_CHUNK == PROJ_COLS and all(s % SUBLANES == 0 for s in WPREP_SRC)


def _wprep_kernel(src_ref, wt_ref, wp_ref):
    del src_ref
    wp_ref[...] = wt_ref[0].T.astype(BF16)


def _wprep(w_in):
    L, K, N = w_in.shape
    assert N == IN_TOTAL
    wt = jnp.swapaxes(w_in, 1, 2)
    w_perm = pl.pallas_call(
        _wprep_kernel,
        grid_spec=pltpu.PrefetchScalarGridSpec(
            num_scalar_prefetch=1,
            grid=(L, len(WPREP_SRC)),
            in_specs=[pl.BlockSpec((pl.Element(1), pl.Element(WPREP_CHUNK), pl.Element(K)),
                                   lambda l, c, src: (l, pl.multiple_of(src[c], SUBLANES), 0))],
            out_specs=pl.BlockSpec((None, K, WPREP_CHUNK), lambda l, c, src: (l, 0, c)),
        ),
        out_shape=jax.ShapeDtypeStruct((L, K, PROJ_COLS), BF16),
        compiler_params=_cparams(("parallel", "parallel")),
        name="wprep",
    )(jnp.asarray(WPREP_SRC, jnp.int32), wt)
    def small_rows(f_ref, dt_ref, o_ref):
        pad = jnp.zeros((AUXT_ROWS - ATTN_HEADS - SSD_HEADS, K), F32)
        o_ref[...] = jnp.concatenate([f_ref[0], dt_ref[0], pad], axis=0).astype(BF16)

    rows_at = lambda start, n: pl.BlockSpec((pl.Element(1), pl.Element(n), pl.Element(K)),
                                            lambda l: (l, start, 0))
    wauxt = pl.pallas_call(
        small_rows,
        grid=(L,),
        in_specs=[rows_at(IN_F, ATTN_HEADS), rows_at(IN_DT, SSD_HEADS)],
        out_specs=pl.BlockSpec((None, AUXT_ROWS, K), lambda l: (l, 0, 0)),
        out_shape=jax.ShapeDtypeStruct((L, AUXT_ROWS, K), BF16),
        compiler_params=_cparams(("parallel",)),
        name="wprep_small",
    )(wt, wt)
    return w_perm, wauxt


def kernel(x, norm_mix, w_in, pool_mix, pool_scale, f_bias, ssd_conv_w, ssd_conv_b, ssd_dt_bias,
           ssd_a_log, ssd_d, ssd_norm, p_pool, p_attn, p_ssd, w_out, norm_ffn, ffn_up, ffn_conv_w,
           ffn_conv_b, ffn_down, norm_final):
    B, S, D = x.shape
    depth = w_in.shape[0]
    T = B * S
    x2d = x.reshape(T, D)
    place = _fox_placement()
    xp = _ssd_expand_matrix()
    fnw = norm_final.reshape(1, D)
    w_perm, wauxt = _wprep(w_in)
    mixw, ppool, pattn, pssd, wout, wup, wdown = (
        w.astype(BF16) for w in (pool_mix, p_pool, p_attn, p_ssd, w_out, ffn_up, ffn_down))
    for l in range(depth):
        proj, aux, auxt = _in_proj(x2d, norm_mix[l].reshape(1, D), w_perm, wauxt, l,
                                   ssd_conv_w[l], ssd_conv_b[l].reshape(1, -1), S)
        qt, ka, vt = _fox_prep(proj, auxt, _pad_lanes(f_bias[l], AUX_F0, AUXT_ROWS).T, place, B, S,
                               ATTN_TILE)
        attn = _flash(qt, ka, vt, B, S, ATTN_TILE)
        dtb128 = _pad_lanes(ssd_dt_bias[l], AUX_DT0)
        alog128 = _pad_lanes(ssd_a_log[l], AUX_DT0)
        yssd = _ssd(proj, aux, auxt, dtb128, alog128,
                    _pad_lanes(ssd_dt_bias[l], AUX_DT0, AUXT_ROWS).T,
                    _pad_lanes(ssd_a_log[l], AUX_DT0, AUXT_ROWS).T,
                    jnp.repeat(ssd_d[l], SSD_HEAD_DIM).reshape(1, -1), ssd_norm[l].reshape(1, -1),
                    xp, B, S)
        x2d = _merge(x2d, proj, attn, yssd, mixw, pool_scale[l].reshape(1, -1),
                     ppool, pattn, pssd, wout, l, B, S)
        x2d = _ffn(x2d, norm_ffn[l].reshape(1, D), wup, ffn_conv_w[l],
                   ffn_conv_b[l].reshape(1, -1), wdown, fnw, l, B, S,
                   final_norm=(l == depth - 1))
    return x2d.reshape(B, S, D)
```

```python
import functools

import jax
import jax.numpy as jnp
from jax import lax
from jax.experimental import pallas as pl
from jax.experimental.pallas import tpu as pltpu

F32 = jnp.float32
BF16 = jnp.bfloat16

D_MODEL = 1024
NORM_EPS = 1e-6
POOL_WINDOWS = (2, 4, 8, 16)
POOL_WIDTH = 512
ATTN_HEADS = 8
ATTN_HEAD_DIM = 64
ATTN_WIDTH = 512
SSD_HEADS = 16
SSD_HEAD_DIM = 64
SSD_WIDTH = 1024
SSD_GROUPS = 2
SSD_STATE = 128
SSD_CONV = 4
SSD_CHUNK = 128
SSD_CONV_CH = 1536
FFN_DIM = 2816
FFN_CONV = 3

LANES = 128
SUBLANES = 8

COL_GATES = 0
COL_Z = 3072
COL_BC = 4096
COL_POOL = 4608
COL_XLO = 5120
COL_Q = 5632
COL_XHI = 6144
COL_K = 6656
COL_V = 7168
PROJ_COLS = 7680
BC_WIDTH = 2 * SSD_GROUPS * SSD_STATE
X_HALF = SSD_WIDTH // 2
PROJ_TILE = 1536
PROJ_CHUNK = 256
TILE_ZBC = COL_Z // PROJ_TILE
TILE_POOLXQ = COL_POOL // PROJ_TILE
TILE_XKV = COL_XHI // PROJ_TILE
assert COL_Z == TILE_ZBC * PROJ_TILE and COL_BC == COL_Z + SSD_WIDTH
assert COL_POOL == TILE_POOLXQ * PROJ_TILE and COL_XLO == COL_POOL + POOL_WIDTH
assert COL_XHI == TILE_XKV * PROJ_TILE and COL_Q == COL_XLO + X_HALF
AUX_F0 = 0
AUX_DT0 = 8
AUXT_ROWS = 32

VMEM_LIMIT = 56 * 1024 * 1024


def _cparams(sem):
    return pltpu.CompilerParams(dimension_semantics=sem, vmem_limit_bytes=VMEM_LIMIT)


def _split3(x):
    hi = x.astype(BF16).astype(F32)
    r1 = x - hi
    mid = r1.astype(BF16).astype(F32)
    lo = (r1 - mid).astype(BF16).astype(F32)
    return hi, mid, lo


def _pack3(x, lane, stride):
    hi, mid, lo = _split3(x)
    packed = jnp.where(lane < stride, hi,
                       jnp.where(lane < 2 * stride, pltpu.roll(mid, stride, 1),
                                 pltpu.roll(lo, 2 * stride, 1)))
    return packed.astype(BF16)


def _silu(x):
    h = 0.5 * x
    return h + h * jnp.tanh(h)


def _inproj_kernel(x_ref, nw_ref, w_ref, wauxt_ref, cw_ref, cb_ref,
                   out_ref, aux_ref, auxt_ref, u_ref, halo_ref, acc_ref, *, tiles_per_seq):
    i = pl.program_id(0)
    j = pl.program_id(1)
    tm = x_ref.shape[0]
    halo = SUBLANES

    def conv_silu(acc, wcols):
        prev = halo_ref[:, wcols]
        prev = jnp.where(i % tiles_per_seq == 0, jnp.zeros_like(prev), prev)
        ext = jnp.concatenate([prev, acc], axis=0)
        halo_ref[:, wcols] = acc[tm - halo:tm, :]
        ext1 = pltpu.roll(ext, 1, 0)
        near = cw_ref[3:4, wcols] * ext + cw_ref[2:3, wcols] * ext1
        far = cw_ref[1:2, wcols] * ext + cw_ref[0:1, wcols] * ext1
        conv = (near + pltpu.roll(far, 2, 0))[halo:, :] + cb_ref[:, wcols]
        return _silu(conv)

    def tile(epilogues, lhs=None, order=None):
        order = list(range(len(epilogues))) if order is None else order
        assert sorted(order) == list(range(len(epilogues)))

        def chunk_dot(n):
            c = order[n]
            acc_ref[n % 2] = jnp.dot(u_ref[...] if lhs is None else lhs,
                                     w_ref[:, c * PROJ_CHUNK:(c + 1) * PROJ_CHUNK],
                                     preferred_element_type=F32)

        chunk_dot(0)
        for n, c in enumerate(order):
            if n + 1 < len(order):
                chunk_dot(n + 1)
            out_ref[:, c * PROJ_CHUNK:(c + 1) * PROJ_CHUNK] = epilogues[c](
                acc_ref[n % 2]).astype(out_ref.dtype)

    def conv_at(off):
        return lambda acc: conv_silu(acc, slice(off, off + PROJ_CHUNK))

    plain = lambda acc: acc
    nchunk = out_ref.shape[1] // PROJ_CHUNK

    @pl.when(j == 0)
    def _():
        x = x_ref[...]
        ms = jnp.mean(x * x, axis=-1, keepdims=True)
        u = (x * lax.rsqrt(ms + NORM_EPS) * nw_ref[...]).astype(BF16)
        u_ref[...] = u
        auxt = lax.dot_general(wauxt_ref[...], u, (((1,), (1,)), ((), ())),
                               preferred_element_type=F32)
        auxt_ref[...] = auxt
        aux_ref[...] = jnp.concatenate(
            [auxt, jnp.zeros((LANES - AUXT_ROWS, tm), F32)], axis=0).T
        tile([plain] * nchunk, lhs=u)

    @pl.when((j != 0) & (j != TILE_ZBC) & (j != TILE_POOLXQ) & (j != TILE_XKV))
    def _():
        tile([plain] * nchunk)

    def convs(first, width):
        return [conv_at(first + k * PROJ_CHUNK) for k in range(width // PROJ_CHUNK)]

    def plains(width):
        return [plain] * (width // PROJ_CHUNK)

    assert nchunk == 6 and BC_WIDTH == X_HALF == 2 * PROJ_CHUNK

    @pl.when(j == TILE_ZBC)
    def _():
        tile([_silu] * (SSD_WIDTH // PROJ_CHUNK) + convs(SSD_WIDTH, BC_WIDTH),
             order=[4, 0, 1, 5, 2, 3])

    @pl.when(j == TILE_POOLXQ)
    def _():
        tile(plains(POOL_WIDTH) + convs(0, X_HALF) + plains(ATTN_WIDTH), order=[2, 0, 1, 3, 4, 5])

    @pl.when(j == TILE_XKV)
    def _():
        tile(convs(X_HALF, X_HALF) + plains(2 * ATTN_WIDTH), order=[0, 2, 3, 1, 4, 5])


def _in_proj(x2d, norm_w, w_perm, wauxt, layer, cw, cb, S, tm=1024, tn=PROJ_TILE):
    T = x2d.shape[0]
    return pl.pallas_call(
        functools.partial(_inproj_kernel, tiles_per_seq=S // tm),
        grid=(T // tm, PROJ_COLS // tn),
        in_specs=[
            pl.BlockSpec((tm, D_MODEL), lambda i, j: (i, 0)),
            pl.BlockSpec((1, D_MODEL), lambda i, j: (0, 0)),
            pl.BlockSpec((None, D_MODEL, tn), lambda i, j: (layer, 0, j)),
            pl.BlockSpec((None, AUXT_ROWS, D_MODEL), lambda i, j: (layer, 0, 0)),
            pl.BlockSpec((SSD_CONV, SSD_CONV_CH), lambda i, j: (0, 0)),
            pl.BlockSpec((1, SSD_CONV_CH), lambda i, j: (0, 0)),
        ],
        out_specs=[
            pl.BlockSpec((tm, tn), lambda i, j: (i, j)),
            pl.BlockSpec((tm, LANES), lambda i, j: (i, 0)),
            pl.BlockSpec((AUXT_ROWS, tm), lambda i, j: (0, i)),
        ],
        out_shape=[
            jax.ShapeDtypeStruct((T, PROJ_COLS), BF16),
            jax.ShapeDtypeStruct((T, LANES), F32),
            jax.ShapeDtypeStruct((AUXT_ROWS, T), F32),
        ],
        scratch_shapes=[pltpu.VMEM((tm, D_MODEL), BF16),
                        pltpu.VMEM((SUBLANES, SSD_CONV_CH), F32),
                        pltpu.VMEM((2, tm, PROJ_CHUNK), F32)],
        compiler_params=_cparams(("arbitrary", "arbitrary")),
        name="in_proj",
    )(x2d, norm_w, w_perm, wauxt, cw, cb)


AUG = 2 * ATTN_HEAD_DIM
ATTN_TILE = 512
FLASH_HEADS = 4
VT_ROWS = ATTN_HEAD_DIM + 16


def _foxprep_kernel(q_ref, k_ref, v_ref, auxt_ref, fbt_ref, triu_ref, place_ref,
                    qt_ref, ka_ref, vt_ref, carry_ref):
    ts = q_ref.shape[0]
    nh = ATTN_HEADS
    R = AUXT_ROWS

    @pl.when(pl.program_id(1) == 0)
    def _():
        carry_ref[...] = jnp.zeros_like(carry_ref)

    logf = jax.nn.log_sigmoid(auxt_ref[...] + fbt_ref[...])
    parts = jnp.concatenate(_split3(logf), axis=0).astype(BF16)
    cs = jnp.dot(parts, triu_ref[...], preferred_element_type=F32)
    ct = cs[0:R] + cs[R:2 * R] + cs[2 * R:3 * R] + carry_ref[...]
    carry_ref[...] = ct[:, ts - 1:ts]

    ch, cm, cl = _split3(ct)
    stacked = jnp.concatenate([ch[0:nh], cm[0:nh], cl[0:nh],
                               jnp.zeros((LANES - 3 * nh, ts), F32)], axis=0)
    aug = jnp.dot(stacked.T.astype(BF16), place_ref[...], preferred_element_type=F32)
    lane = lax.broadcasted_iota(jnp.int32, (ts, LANES), 1)
    is_q = lane < ATTN_HEAD_DIM
    is_pos = lane < ATTN_HEAD_DIM + 3
    ones_hi = jnp.where((lane >= ATTN_HEAD_DIM + 3) & (lane < ATTN_HEAD_DIM + 6), 1.0, 0.0)
    ones_lo = jnp.where((lane >= ATTN_HEAD_DIM) & is_pos, 1.0, 0.0)
    scale = ATTN_HEAD_DIM ** -0.5
    row_id = lax.broadcasted_iota(jnp.int32, (VT_ROWS - ATTN_HEAD_DIM, ts), 0)
    ones_rows = jnp.where(row_id == 0, 1.0, 0.0)
    for h in range(ATTN_HEADS):
        pair = slice((h // 2) * LANES, (h // 2 + 1) * LANES)
        blk = slice(h * LANES, (h + 1) * LANES)
        qp = q_ref[:, pair].astype(F32)
        kp = k_ref[:, pair].astype(F32)
        if h % 2 == 1:
            qp = pltpu.roll(qp, ATTN_HEAD_DIM, 1)
            kp = pltpu.roll(kp, ATTN_HEAD_DIM, 1)
        qa = jnp.where(is_q, qp * scale, jnp.where(is_pos, aug[:, blk], ones_hi))
        qt_ref[0, h] = qa.T.astype(BF16)
        ka_ref[0, h] = jnp.where(is_q, kp, jnp.where(is_pos, ones_lo, aug[:, blk])).astype(BF16)
        vt = v_ref[:, pair].astype(F32).T
        vh = vt[(h % 2) * ATTN_HEAD_DIM:(h % 2 + 1) * ATTN_HEAD_DIM, :]
        vt_ref[0, h, 0] = jnp.concatenate([vh, ones_rows], axis=0).astype(BF16)


def _fox_prep(proj, auxt, fbt, place, B, S, ts):
    ns = S // ts
    triu = (jnp.arange(ts)[:, None] <= jnp.arange(ts)[None, :]).astype(BF16)
    return pl.pallas_call(
        _foxprep_kernel,
        grid=(B, ns),
        in_specs=[
            pl.BlockSpec((ts, ATTN_WIDTH), lambda b, i: (b * ns + i, COL_Q // ATTN_WIDTH)),
            pl.BlockSpec((ts, ATTN_WIDTH), lambda b, i: (b * ns + i, COL_K // ATTN_WIDTH)),
            pl.BlockSpec((ts, ATTN_WIDTH), lambda b, i: (b * ns + i, COL_V // ATTN_WIDTH)),
            pl.BlockSpec((AUXT_ROWS, ts), lambda b, i: (0, b * ns + i)),
            pl.BlockSpec((AUXT_ROWS, 1), lambda b, i: (0, 0)),
            pl.BlockSpec((ts, ts), lambda b, i: (0, 0)),
            pl.BlockSpec((LANES, ATTN_HEADS * AUG), lambda b, i: (0, 0)),
        ],
        out_specs=[
            pl.BlockSpec((1, ATTN_HEADS, AUG, ts), lambda b, i: (b, 0, 0, i)),
            pl.BlockSpec((1, ATTN_HEADS, ts, AUG), lambda b, i: (b, 0, i, 0)),
            pl.BlockSpec((1, ATTN_HEADS, 1, VT_ROWS, ts), lambda b, i: (b, 0, i, 0, 0)),
        ],
        out_shape=[
            jax.ShapeDtypeStruct((B, ATTN_HEADS, AUG, S), BF16),
            jax.ShapeDtypeStruct((B, ATTN_HEADS, S, AUG), BF16),
            jax.ShapeDtypeStruct((B, ATTN_HEADS, ns, VT_ROWS, ts), BF16),
        ],
        scratch_shapes=[pltpu.VMEM((AUXT_ROWS, 1), F32)],
        compiler_params=_cparams(("parallel", "arbitrary")),
        name="fox_prep",
    )(proj, proj, proj, auxt, fbt, triu, place)


def _fox_placement():
    rows = jnp.arange(LANES)
    part = rows // ATTN_HEADS
    head = rows % ATTN_HEADS
    cols = jnp.arange(ATTN_HEADS * AUG)
    chead = cols // AUG
    clane = cols % AUG
    valid = (part[:, None] < 3) & (head[:, None] == chead[None, :])
    plus = valid & (clane[None, :] == ATTN_HEAD_DIM + part[:, None])
    minus = valid & (clane[None, :] == ATTN_HEAD_DIM + 3 + part[:, None])
    return (plus.astype(F32) - minus.astype(F32)).astype(BF16)


def _flash_kernel(qt_ref, k_ref, vt_ref, o_ref, m_ref, acc_ref, sa_ref, sb_ref):
    i = pl.program_id(2)
    t = qt_ref.shape[3]
    m_ref[...] = jnp.full_like(m_ref, -jnp.inf)
    acc_ref[...] = jnp.zeros_like(acc_ref)

    def scores(dst_ref, j):
        for h in range(FLASH_HEADS):
            k = k_ref[0, h, pl.ds(pl.multiple_of(j * t, t), t), :]
            dst_ref[h] = jnp.dot(k, qt_ref[0, h], preferred_element_type=F32)

    def softmax_pv(src_ref, j, masked):
        for h in range(FLASH_HEADS):
            s = src_ref[h]
            if masked:
                row = lax.broadcasted_iota(jnp.int32, (t, t), 0)
                col = lax.broadcasted_iota(jnp.int32, (t, t), 1)
                s = jnp.where(row <= col, s, -jnp.inf)
            m_prev = m_ref[h]
            m_next = jnp.maximum(m_prev, jnp.max(s, axis=0, keepdims=True))
            alpha = jnp.exp(m_prev - m_next)
            p = jnp.exp(s - m_next).astype(BF16)
            acc_ref[h] = alpha * acc_ref[h] + jnp.dot(vt_ref[0, h, j], p,
                                                      preferred_element_type=F32)
            m_ref[h] = m_next

    last = jnp.maximum(i - 1, 0)
    scores(sa_ref, i)
    scores(sb_ref, 0)
    softmax_pv(sa_ref, i, True)

    def body(jj, carry):
        j0 = 2 * jj
        scores(sa_ref, j0 + 1)
        softmax_pv(sb_ref, j0, False)
        scores(sb_ref, jnp.minimum(j0 + 2, last))
        softmax_pv(sa_ref, j0 + 1, False)
        return carry

    lax.fori_loop(0, i // 2, body, 0)

    @pl.when(i % 2 == 1)
    def _():
        softmax_pv(sb_ref, i - 1, False)

    outs = []
    for h in range(FLASH_HEADS):
        acc = acc_ref[h]
        outs.append(acc[0:ATTN_HEAD_DIM, :] / acc[ATTN_HEAD_DIM:ATTN_HEAD_DIM + 1, :])
    o_ref[...] = jnp.concatenate(outs, axis=0).T.astype(o_ref.dtype)


def _flash(qt, ka, vt, B, S, t):
    n = S // t
    T = B * S
    g = FLASH_HEADS
    return pl.pallas_call(
        _flash_kernel,
        grid=(B, ATTN_HEADS // g, n),
        in_specs=[
            pl.BlockSpec((1, g, AUG, t), lambda b, h, i: (b, h, 0, i)),
            pl.BlockSpec((1, g, S, AUG), lambda b, h, i: (b, h, 0, 0)),
            pl.BlockSpec((1, g, n, VT_ROWS, t), lambda b, h, i: (b, h, 0, 0, 0)),
        ],
        out_specs=pl.BlockSpec((t, g * ATTN_HEAD_DIM), lambda b, h, i: (b * n + i, h)),
        out_shape=jax.ShapeDtypeStruct((T, ATTN_WIDTH), BF16),
        scratch_shapes=[
            pltpu.VMEM((g, 1, t), F32),
            pltpu.VMEM((g, VT_ROWS, t), F32),
            pltpu.VMEM((g, t, t), F32),
            pltpu.VMEM((g, t, t), F32),
        ],
        compiler_params=_cparams(("parallel", "parallel", "arbitrary")),
        name="flash",
    )(qt, ka, vt)


PACK_STRIDE = 32
LOG2E = 1.4426950408889634


def _ssd_kernel(zs_ref, xlo_ref, xhi_ref, bc_ref, aux_ref, auxt_ref, dtb_ref, alog_ref,
                dtbt_ref, alogt_ref, dexp_ref, nw_ref, xp_ref,
                y_ref, state_ref):
    ts = zs_ref.shape[0]
    L = SSD_CHUNK

    @pl.when(pl.program_id(1) == 0)
    def _():
        state_ref[...] = jnp.zeros_like(state_ref)

    dt =jax.nn.softplus(aux_ref[...] + dtb_ref[...])
    a = dt * (-jnp.exp(alog_ref[...]))
    dtt = jax.nn.softplus(auxt_ref[...] + dtbt_ref[...])
    at = dtt * (-jnp.exp(alogt_ref[...]))

    row = lax.broadcasted_iota(jnp.int32, (L, L), 0)
    col = lax.broadcasted_iota(jnp.int32, (L, L), 1)
    causal = col <= row
    tril = causal.astype(BF16)
    triu = (row <= col).astype(BF16)
    lane = lax.broadcasted_iota(jnp.int32, (L, LANES), 1)
    first_half = lane < SSD_HEAD_DIM
    xp = xp_ref[...]
    gw = SSD_WIDTH // SSD_GROUPS
    hpg = SSD_HEADS // SSD_GROUPS

    for c in range(ts // L):
        r = slice(c * L, (c + 1) * L)
        a_c = a[r]
        ah, am, al = _split3(a_c)
        a3 = jnp.concatenate([ah, am, al], axis=1).astype(BF16)
        cs3 = jnp.dot(tril, a3, preferred_element_type=F32)
        a_cs = cs3[:, :LANES] + cs3[:, LANES:2 * LANES] + cs3[:, 2 * LANES:]
        th, tm_, tl = _split3(at[:, r])
        t3 = jnp.concatenate([th, tm_, tl], axis=0).astype(BF16)
        cst = jnp.dot(t3, triu, preferred_element_type=F32)
        a_cst = cst[0:AUXT_ROWS] + cst[AUXT_ROWS:2 * AUXT_ROWS] + cst[2 * AUXT_ROWS:]

        a_l2 = a_cs * LOG2E
        a_l2t = a_cst * LOG2E
        e_small = jnp.exp(a_cs)
        w_small = jnp.exp(a_cs[L - 1:L, :] - a_cs) * dt[r]
        expand = lambda v: jnp.dot(_pack3(v, lane, PACK_STRIDE), xp, preferred_element_type=F32)
        dt_exp = expand(dt[r])
        w_exp = expand(w_small)
        e_exp = expand(e_small)

        x_c = jnp.concatenate([xlo_ref[r, :], xhi_ref[r, :]], axis=1).astype(F32)
        xd = (x_c * dt_exp).astype(BF16)
        xw = (x_c * w_exp).astype(BF16)
        y_parts = []
        for g in range(SSD_GROUPS):
            b_gb = bc_ref[r, g * SSD_STATE:(g + 1) * SSD_STATE]
            c_gb = bc_ref[r, (SSD_GROUPS + g) * SSD_STATE:(SSD_GROUPS + g + 1) * SSD_STATE]
            cb = lax.dot_general(c_gb, b_gb, (((1,), (1,)), ((), ())),
                                 preferred_element_type=F32)
            gcols = slice(g * gw, (g + 1) * gw)
            st = state_ref[:, gcols]
            y_off = jnp.dot(c_gb, st.astype(BF16), preferred_element_type=F32) * e_exp[:, gcols]
            state_ref[:, gcols] = e_exp[L - 1:L, gcols] * st + jnp.dot(
                b_gb.astype(F32).T.astype(BF16), xw[:, gcols], preferred_element_type=F32)
            for pr in range(hpg // 2):
                ms = []
                for e in range(2):
                    h = g * hpg + pr * 2 + e
                    seg = a_l2[:, AUX_DT0 + h:AUX_DT0 + h + 1] - a_l2t[AUX_DT0 + h:AUX_DT0 + h + 1, :]
                    lm = jnp.exp2(jnp.where(causal, seg, -jnp.inf))
                    ms.append((cb * lm).astype(BF16))
                lhs = jnp.concatenate(ms, axis=1)
                p0 = g * gw + pr * LANES
                xpair = xd[:, p0:p0 + LANES]
                zero = jnp.zeros_like(xpair)
                rhs = jnp.concatenate([jnp.where(first_half, xpair, zero),
                                       jnp.where(first_half, zero, xpair)], axis=0)
                y_parts.append(jnp.dot(lhs, rhs, preferred_element_type=F32)
                               + y_off[:, pr * LANES:(pr + 1) * LANES])
        y = jnp.concatenate(y_parts, axis=1) + dexp_ref[...] * x_c
        y = y * zs_ref[r, :].astype(F32)
        outs = []
        for g in range(SSD_GROUPS):
            yg = y[:, g * gw:(g + 1) * gw]
            yg = yg * lax.rsqrt(jnp.mean(yg * yg, axis=-1, keepdims=True) + NORM_EPS)
            outs.append(yg)
        y_ref[r, :] = (jnp.concatenate(outs, axis=1) * nw_ref[...]).astype(y_ref.dtype)


def _ssd(proj, aux, auxt, dtb128, alog128, dtbt, alogt, dexp, nw, xp, B, S, ts=512):
    ns = S // ts
    T = B * S
    return pl.pallas_call(
        _ssd_kernel,
        grid=(B, ns),
        in_specs=[
            pl.BlockSpec((ts, SSD_WIDTH), lambda b, i: (b * ns + i, COL_Z // SSD_WIDTH)),
            pl.BlockSpec((ts, X_HALF), lambda b, i: (b * ns + i, COL_XLO // X_HALF)),
            pl.BlockSpec((ts, X_HALF), lambda b, i: (b * ns + i, COL_XHI // X_HALF)),
            pl.BlockSpec((ts, BC_WIDTH), lambda b, i: (b * ns + i, COL_BC // BC_WIDTH)),
            pl.BlockSpec((ts, LANES), lambda b, i: (b * ns + i, 0)),
            pl.BlockSpec((AUXT_ROWS, ts), lambda b, i: (0, b * ns + i)),
            pl.BlockSpec((1, LANES), lambda b, i: (0, 0)),
            pl.BlockSpec((1, LANES), lambda b, i: (0, 0)),
            pl.BlockSpec((AUXT_ROWS, 1), lambda b, i: (0, 0)),
            pl.BlockSpec((AUXT_ROWS, 1), lambda b, i: (0, 0)),
            pl.BlockSpec((1, SSD_WIDTH), lambda b, i: (0, 0)),
            pl.BlockSpec((1, SSD_WIDTH), lambda b, i: (0, 0)),
            pl.BlockSpec((LANES, SSD_WIDTH), lambda b, i: (0, 0)),
        ],
        out_specs=pl.BlockSpec((ts, SSD_WIDTH), lambda b, i: (b * ns + i, 0)),
        out_shape=jax.ShapeDtypeStruct((T, SSD_WIDTH), BF16),
        scratch_shapes=[pltpu.VMEM((SSD_STATE, SSD_WIDTH), F32)],
        compiler_params=_cparams(("parallel", "arbitrary")),
        name="ssd",
    )(proj, proj, proj, proj, aux, auxt, dtb128, alog128, dtbt, alogt, dexp, nw, xp)


def _ssd_expand_matrix():
    rows = jnp.arange(LANES)
    slot = rows % PACK_STRIDE
    part = rows // PACK_STRIDE
    head = slot - AUX_DT0
    cols = jnp.arange(SSD_WIDTH) // SSD_HEAD_DIM
    valid = (part < 3) & (head >= 0) & (head < SSD_HEADS)
    return jnp.where(valid[:, None] & (head[:, None] == cols[None, :]), 1.0, 0.0).astype(BF16)


POOL_HALO = 16


def _merge_kernel(x_ref, gates_ref, pool_ref, prev_ref, attn_ref, ssd_ref, mixw_ref, pscale_ref,
                  ppool_ref, pattn_ref, pssd_ref, wout_ref, o_ref):
    tm = x_ref.shape[0]
    i = pl.program_id(1)
    gdim = POOL_WIDTH // len(POOL_WINDOWS)

    prev = prev_ref[...].astype(F32)
    prev = jnp.where(i == 0, jnp.zeros_like(prev), prev)
    head_pos = i * tm + lax.broadcasted_iota(jnp.int32, (POOL_HALO, gdim), 0)
    ys = []
    for g, w in enumerate(POOL_WINDOWS):
        assert w <= POOL_HALO and w & (w - 1) == 0
        cols = slice(g * gdim, (g + 1) * gdim)
        v = pool_ref[:, cols].astype(F32)
        acc = jnp.concatenate([prev[:, cols], v], axis=0)
        span = 1
        while span < w:
            acc = acc + pltpu.roll(acc, span, 0)
            span *= 2
        inv_cnt = jnp.concatenate(
            [1.0 / jnp.minimum(head_pos + 1, w).astype(F32),
             jnp.full((tm - POOL_HALO, gdim), 1.0 / w, F32)], axis=0)
        d = (acc[POOL_HALO:, :] * inv_cnt - v).astype(BF16)
        ys.append(jnp.dot(d, mixw_ref[g], preferred_element_type=F32))
    ypre = (jnp.concatenate(ys, axis=1) * pscale_ref[...]).astype(BF16)
    y_pool = jnp.dot(ypre, ppool_ref[...], preferred_element_type=F32)
    y_attn = jnp.dot(attn_ref[...], pattn_ref[...], preferred_element_type=F32)
    y_ssd = jnp.dot(ssd_ref[...], pssd_ref[...], preferred_element_type=F32)
    def gate(k):
        x = gates_ref[:, k * D_MODEL:(k + 1) * D_MODEL].astype(F32)
        return 0.5 * jnp.tanh(0.5 * x) + 0.5

    merged = gate(0) * y_pool + gate(1) * y_attn + gate(2) * y_ssd
    o_ref[...] = x_ref[...] + jnp.dot(merged.astype(BF16), wout_ref[...],
                                      preferred_element_type=F32)


def _merge(x2d, proj, attn, yssd, mixw, pscale, ppool, pattn, pssd, wout, layer, B, S, tm=512):
    ns = S // tm
    T = B * S
    hb = tm // POOL_HALO
    const2 = lambda b, i: (0, 0)
    stacked = lambda *shape: pl.BlockSpec((None,) + shape, lambda b, i: (layer,) + (0,) * len(shape))
    return pl.pallas_call(
        _merge_kernel,
        grid=(B, ns),
        in_specs=[
            pl.BlockSpec((tm, D_MODEL), lambda b, i: (b * ns + i, 0)),
            pl.BlockSpec((tm, 3 * D_MODEL), lambda b, i: (b * ns + i, 0)),
            pl.BlockSpec((tm, POOL_WIDTH), lambda b, i: (b * ns + i, COL_POOL // POOL_WIDTH)),
            pl.BlockSpec((POOL_HALO, POOL_WIDTH),
                         lambda b, i: (jnp.maximum((b * ns + i) * hb - 1, 0), COL_POOL // POOL_WIDTH)),
            pl.BlockSpec((tm, ATTN_WIDTH), lambda b, i: (b * ns + i, 0)),
            pl.BlockSpec((tm, SSD_WIDTH), lambda b, i: (b * ns + i, 0)),
            stacked(len(POOL_WINDOWS), LANES, LANES),
            pl.BlockSpec((1, POOL_WIDTH), const2),
            stacked(POOL_WIDTH, D_MODEL),
            stacked(ATTN_WIDTH, D_MODEL),
            stacked(SSD_WIDTH, D_MODEL),
            stacked(D_MODEL, D_MODEL),
        ],
        out_specs=pl.BlockSpec((tm, D_MODEL), lambda b, i: (b * ns + i, 0)),
        out_shape=jax.ShapeDtypeStruct((T, D_MODEL), F32),
        compiler_params=_cparams(("parallel", "parallel")),
        name="merge",
    )(x2d, proj, proj, proj, attn, yssd, mixw, pscale, ppool, pattn, pssd, wout)


FFN_CHUNK = 256


def _ffn_kernel(x_ref, nw_ref, wup_ref, cw_ref, cb_ref, wd_ref, fnw_ref,
                o_ref, u_ref, act_ref, stage_ref, car_ref, *, final_norm):
    tm = x_ref.shape[0]
    halo = SUBLANES
    nch = FFN_DIM // FFN_CHUNK

    x = x_ref[...]
    ms = jnp.mean(x * x, axis=-1, keepdims=True)
    u_ref[...] = (x * lax.rsqrt(ms + NORM_EPS) * nw_ref[...]).astype(BF16)

    @pl.when(pl.program_id(1) == 0)
    def _():
        car_ref[...] = jnp.zeros_like(car_ref)

    def cols_of(c, half):
        start = half * FFN_DIM + c * FFN_CHUNK
        return slice(start, start + FFN_CHUNK)

    def up(c):
        for half in range(2):
            stage_ref[2 * (c % 2) + half] = jnp.dot(u_ref[...], wup_ref[:, cols_of(c, half)],
                                                    preferred_element_type=F32)

    def conv(c, half):
        cols = cols_of(c, half)
        h = stage_ref[2 * (c % 2) + half]
        ext = jnp.concatenate([car_ref[:, cols], h], axis=0)
        car_ref[:, cols] = h[tm - halo:tm, :]
        out = (cw_ref[2:3, cols] * ext + cw_ref[1:2, cols] * pltpu.roll(ext, 1, 0)
               + cw_ref[0:1, cols] * pltpu.roll(ext, 2, 0))
        return out[halo:, :] + cb_ref[:, cols]

    up(0)
    for c in range(nch):
        if c + 1 < nch:
            up(c + 1)
        act_ref[:, c * FFN_CHUNK:(c + 1) * FFN_CHUNK] = (_silu(conv(c, 0)) * conv(c, 1)).astype(BF16)

    y = x_ref[...] + jnp.dot(act_ref[...], wd_ref[...], preferred_element_type=F32)
    if final_norm:
        ms = jnp.mean(y * y, axis=-1, keepdims=True)
        y = y * lax.rsqrt(ms + NORM_EPS) * fnw_ref[...]
    o_ref[...] = y


def _ffn(x2d, nw, wup, cw, cb, wdown, fnw, layer, B, S, final_norm, tm=512):
    ns = S // tm
    T = B * S
    whole = lambda shape: pl.BlockSpec(shape, lambda b, i: (0, 0))
    stacked = lambda *shape: pl.BlockSpec((None,) + shape, lambda b, i: (layer, 0, 0))
    return pl.pallas_call(
        functools.partial(_ffn_kernel, final_norm=final_norm),
        grid=(B, ns),
        in_specs=[
            pl.BlockSpec((tm, D_MODEL), lambda b, i: (b * ns + i, 0)),
            whole((1, D_MODEL)),
            stacked(D_MODEL, 2 * FFN_DIM),
            whole((FFN_CONV, 2 * FFN_DIM)),
            whole((1, 2 * FFN_DIM)),
            stacked(FFN_DIM, D_MODEL),
            whole((1, D_MODEL)),
        ],
        out_specs=pl.BlockSpec((tm, D_MODEL), lambda b, i: (b * ns + i, 0)),
        out_shape=jax.ShapeDtypeStruct((T, D_MODEL), F32),
        scratch_shapes=[
            pltpu.VMEM((tm, D_MODEL), BF16),
            pltpu.VMEM((tm, FFN_DIM), BF16),
            pltpu.VMEM((4, tm, FFN_CHUNK), F32),
            pltpu.VMEM((SUBLANES, 2 * FFN_DIM), F32),
        ],
        compiler_params=_cparams(("parallel", "arbitrary")),
        name="ffn",
    )(x2d, nw, wup, cw, cb, wdown, fnw)


def _pad_lanes(v, offset, width=LANES):
    return jnp.zeros((1, width), F32).at[0, offset:offset + v.shape[0]].set(v.astype(F32))


IN_POOL, IN_QKV, IN_F, IN_Z, IN_X, IN_BC, IN_DT, IN_GATES = 0, 512, 2048, 2056, 3080, 4104, 4616, 4632
IN_TOTAL = IN_GATES + 3 * D_MODEL


WPREP_CHUNK = 512
WPREP_SRC = tuple(start + WPREP_CHUNK * k
                  for start, width in ((IN_GATES, 3 * D_MODEL), (IN_Z, SSD_WIDTH), (IN_BC, BC_WIDTH),
                                       (IN_POOL, POOL_WIDTH), (IN_X, X_HALF),
                                       (IN_QKV, ATTN_WIDTH), (IN_X + X_HALF, X_HALF),
                                       (IN_QKV + ATTN_WIDTH, 2 * ATTN_WIDTH))
                  for k in range(width // WPREP_CHUNK))
assert len(WPREP_SRC) * WPREP_CHUNK == PROJ_COLS and all(s % SUBLANES == 0 for s in WPREP_SRC)


def _wprep_kernel(src_ref, wt_ref, wp_ref):
    del src_ref
    wp_ref[...] = wt_ref[0].T.astype(BF16)


def _wprep(w_in):
    L, K, N = w_in.shape
    assert N == IN_TOTAL
    wt = jnp.swapaxes(w_in, 1, 2)
    w_perm = pl.pallas_call(
        _wprep_kernel,
        grid_spec=pltpu.PrefetchScalarGridSpec(
            num_scalar_prefetch=1,
            grid=(L, len(WPREP_SRC)),
            in_specs=[pl.BlockSpec((pl.Element(1), pl.Element(WPREP_CHUNK), pl.Element(K)),
                                   lambda l, c, src: (l, pl.multiple_of(src[c], SUBLANES), 0))],
            out_specs=pl.BlockSpec((None, K, WPREP_CHUNK), lambda l, c, src: (l, 0, c)),
        ),
        out_shape=jax.ShapeDtypeStruct((L, K, PROJ_COLS), BF16),
        compiler_params=_cparams(("parallel", "parallel")),
        name="wprep",
    )(jnp.asarray(WPREP_SRC, jnp.int32), wt)
    def small_rows(f_ref, dt_ref, o_ref):
        pad = jnp.zeros((AUXT_ROWS - ATTN_HEADS - SSD_HEADS, K), F32)
        o_ref[...] = jnp.concatenate([f_ref[0], dt_ref[0], pad], axis=0).astype(BF16)

    rows_at = lambda start, n: pl.BlockSpec((pl.Element(1), pl.Element(n), pl.Element(K)),
                                            lambda l: (l, start, 0))
    wauxt = pl.pallas_call(
        small_rows,
        grid=(L,),
        in_specs=[rows_at(IN_F, ATTN_HEADS), rows_at(IN_DT, SSD_HEADS)],
        out_specs=pl.BlockSpec((None, AUXT_ROWS, K), lambda l: (l, 0, 0)),
        out_shape=jax.ShapeDtypeStruct((L, AUXT_ROWS, K), BF16),
        compiler_params=_cparams(("parallel",)),
        name="wprep_small",
    )(wt, wt)
    return w_perm, wauxt


def kernel(x, norm_mix, w_in, pool_mix, pool_scale, f_bias, ssd_conv_w, ssd_conv_b, ssd_dt_bias,
           ssd_a_log, ssd_d, ssd_norm, p_pool, p_attn, p_ssd, w_out, norm_ffn, ffn_up, ffn_conv_w,
           ffn_conv_b, ffn_down, norm_final):
    B, S, D = x.shape
    depth = w_in.shape[0]
    T = B * S
    x2d = x.reshape(T, D)
    place = _fox_placement()
    xp = _ssd_expand_matrix()
    fnw = norm_final.reshape(1, D)
    w_perm, wauxt = _wprep(w_in)
    mixw, ppool, pattn, pssd, wout, wup, wdown = (
        w.astype(BF16) for w in (pool_mix, p_pool, p_attn, p_ssd, w_out, ffn_up, ffn_down))
    for l in range(depth):
        proj, aux, auxt = _in_proj(x2d, norm_mix[l].reshape(1, D), w_perm, wauxt, l,
                                   ssd_conv_w[l], ssd_conv_b[l].reshape(1, -1), S)
        qt, ka, vt = _fox_prep(proj, auxt, _pad_lanes(f_bias[l], AUX_F0, AUXT_ROWS).T, place, B, S,
                               ATTN_TILE)
        attn = _flash(qt, ka, vt, B, S, ATTN_TILE)
        dtb128 = _pad_lanes(ssd_dt_bias[l], AUX_DT0)
        alog128 = _pad_lanes(ssd_a_log[l], AUX_DT0)
        yssd = _ssd(proj, aux, auxt, dtb128, alog128,
                    _pad_lanes(ssd_dt_bias[l], AUX_DT0, AUXT_ROWS).T,
                    _pad_lanes(ssd_a_log[l], AUX_DT0, AUXT_ROWS).T,
                    jnp.repeat(ssd_d[l], SSD_HEAD_DIM).reshape(1, -1), ssd_norm[l].reshape(1, -1),
                    xp, B, S)
        x2d = _merge(x2d, proj, attn, yssd, mixw, pool_scale[l].reshape(1, -1),
                     ppool, pattn, pssd, wout, l, B, S)
        x2d = _ffn(x2d, norm_ffn[l].reshape(1, D), wup, ffn_conv_w[l],
                   ffn_conv_b[l].reshape(1, -1), wdown, fnw, l, B, S,
                   final_norm=(l == depth - 1))
    return x2d.reshape(B, S, D)
```
